```python
import math
import jax, jax.numpy as jnp
from jax import lax
import numpy as np

D_MODEL = 1024
BATCH = 8
SEQ = 4096
DEPTH = 1
DEC_BATCH = 8
DEC_SEQ = 64
PAST_LEN = 2048

CHUNK = 64
QBLK = 128
DA_HEADS = 8
DA_HEAD_DIM = 64
DA_WIDTH = DA_HEADS * 2 * DA_HEAD_DIM
MLA_HEADS = 16
MLA_Q_LORA = 256
MLA_KV_LORA = 128
MLA_NOPE = 64
MLA_ROPE = 32
MLA_V = 64
MLA_WIDTH = MLA_HEADS * MLA_V
ROPE_THETA = 10000.0
REL_BUCKETS = 32
REL_MAX_DIST = 128
PEER_HEADS = 8
PEER_N_KEYS = 128
PEER_N_EXPERTS = PEER_N_KEYS * PEER_N_KEYS
PEER_D_KEY = 128
PEER_TOPK = 16
PEER_BLOCK = 128
IN_WIDTH = 3 * DA_WIDTH + MLA_Q_LORA + MLA_KV_LORA + MLA_ROPE + 2 * D_MODEL
NORM_EPS = 1e-6
NEG_INF = -1e30

kernel_name = "hybrid_diffattn_mla_peer_streaming_step"


def rmsnorm(x, g):
    xf = x.astype(jnp.float32)
    y = xf * lax.rsqrt(jnp.mean(xf * xf, axis=-1, keepdims=True) + NORM_EPS)
    return (y * g.astype(jnp.float32)).astype(x.dtype)


def rope(x, pos):
    half = x.shape[-1] // 2
    inv = ROPE_THETA ** (-jnp.arange(half, dtype=jnp.float32) / half)
    ang = pos.astype(jnp.float32)[:, None] * inv
    ang = ang.reshape(ang.shape[0], *([1] * (x.ndim - 3)), half)
    cos, sin = jnp.cos(ang), jnp.sin(ang)
    xf = x.astype(jnp.float32)
    x1, x2 = xf[..., :half], xf[..., half:]
    return jnp.concatenate([x1 * cos - x2 * sin, x1 * sin + x2 * cos], axis=-1).astype(x.dtype)


def t5_bucket(rel):
    nb = REL_BUCKETS // 2
    max_exact = nb // 2
    ret = jnp.where(rel > 0, nb, 0)
    n = jnp.abs(rel)
    large = max_exact + (jnp.log(jnp.maximum(n, 1).astype(jnp.float32) / max_exact)
                         / math.log(REL_MAX_DIST / max_exact) * (nb - max_exact)).astype(jnp.int32)
    large = jnp.minimum(large, nb - 1)
    return ret + jnp.where(n < max_exact, n, large)


def chunk_mask(q_pos, k_pos):
    return (k_pos // CHUNK)[None, :] <= (q_pos // CHUNK)[:, None]


def over_query_blocks(fn, qs, q_pos):
    B, S = qs[0].shape[:2]
    if S <= QBLK or S % QBLK:
        return fn(*qs, q_pos)
    nb = S // QBLK
    qb = tuple(jnp.moveaxis(q.reshape(B, nb, QBLK, *q.shape[2:]), 1, 0) for q in qs)
    out = lax.map(lambda a: fn(*a[0], a[1]), (qb, q_pos.reshape(nb, QBLK)))
    return jnp.moveaxis(out, 0, 1).reshape(B, S, *out.shape[3:])


def diff_attention(q, k, v, q_pos, k_pos, lam, rel_bias):
    s = jnp.einsum('bqhcd,bkhcd->bchqk', q, k).astype(jnp.float32) * (DA_HEAD_DIM ** -0.5)
    bias = jnp.transpose(rel_bias[t5_bucket(k_pos[None, :] - q_pos[:, None])], (2, 0, 1)).astype(jnp.float32)
    s = jnp.where(chunk_mask(q_pos, k_pos), s + bias, NEG_INF)
    p = jax.nn.softmax(s, axis=-1)
    pd = p[:, 0] - lam * p[:, 1]
    return jnp.einsum('bhqk,bkhe->bqhe', pd.astype(v.dtype), v)


def mla_attention(q_lat, q_pe, ckv, kpe, q_pos, k_pos):
    s = (jnp.einsum('bqhr,bkr->bhqk', q_lat, ckv)
         + jnp.einsum('bqhe,bke->bhqk', q_pe, kpe)).astype(jnp.float32) * ((MLA_NOPE + MLA_ROPE) ** -0.5)
    s = jnp.where(chunk_mask(q_pos, k_pos), s, NEG_INF)
    p = jax.nn.softmax(s, axis=-1)
    return jnp.einsum('bhqk,bkr->bqhr', p.astype(ckv.dtype), ckv)


def peer_tokens(h, w_q, keys, u, v):
    T = h.shape[0]
    q = (h @ w_q).reshape(T, PEER_HEADS, 2, PEER_D_KEY // 2)
    s = jnp.einsum('thcd,hcnd->thcn', q, keys).astype(jnp.float32)
    s1, i1 = lax.top_k(s[:, :, 0], PEER_TOPK)
    s2, i2 = lax.top_k(s[:, :, 1], PEER_TOPK)
    cand = (s1[..., :, None] + s2[..., None, :]).reshape(T, PEER_HEADS, PEER_TOPK * PEER_TOPK)
    cidx = (i1[..., :, None] * PEER_N_KEYS + i2[..., None, :]).reshape(T, PEER_HEADS, PEER_TOPK * PEER_TOPK)
    top, sel = lax.top_k(cand, PEER_TOPK)
    idx = jnp.take_along_axis(cidx, sel, axis=-1)
    g = jax.nn.softmax(top, axis=-1).astype(h.dtype)
    act = jax.nn.gelu(jnp.einsum('td,thkd->thk', h, u[idx]), approximate=False)
    return jnp.einsum('thk,thkd->td', g * act, v[idx])


def peer_ffn(h, w_q, keys, u, v):
    B, S, D = h.shape
    t = h.reshape(B * S, D)
    T = t.shape[0]
    if T > PEER_BLOCK and T % PEER_BLOCK == 0:
        out = lax.map(lambda blk: peer_tokens(blk, w_q, keys, u, v), t.reshape(T // PEER_BLOCK, PEER_BLOCK, D))
        out = out.reshape(T, D)
    else:
        out = peer_tokens(t, w_q, keys, u, v)
    return out.reshape(B, S, D)


def trunk_layer(x, pos, past, layer_idx, rel_bias, norm_mix, w_in, diff_lambda, diff_subln,
                mla_q_norm, mla_w_uq, mla_kv_norm, mla_w_uk, mla_w_uv,
                w_branch_a, w_branch_b, w_out, norm_ffn, peer_w_q, peer_keys, peer_u, peer_v):
    B, S, _ = x.shape
    h = rmsnorm(x, norm_mix)
    z = h @ w_in
    c0 = 3 * DA_WIDTH
    cuts = [DA_WIDTH, 2 * DA_WIDTH, c0, c0 + MLA_Q_LORA, c0 + MLA_Q_LORA + MLA_KV_LORA,
            c0 + MLA_Q_LORA + MLA_KV_LORA + MLA_ROPE]
    zq, zk, zv, zcq, zckv, zkr, zg = jnp.split(z, cuts, axis=-1)
    dq = zq.reshape(B, S, DA_HEADS, 2, DA_HEAD_DIM)
    dk = zk.reshape(B, S, DA_HEADS, 2, DA_HEAD_DIM)
    dv = zv.reshape(B, S, DA_HEADS, 2 * DA_HEAD_DIM)
    gates = jax.nn.sigmoid(zg).reshape(B, S, 2, D_MODEL)
    cq = rmsnorm(zcq, mla_q_norm)
    qh = jnp.einsum('bsr,rhe->bshe', cq, mla_w_uq)
    q_nope = qh[..., :MLA_NOPE]
    q_pe = rope(qh[..., MLA_NOPE:], pos)
    q_lat = jnp.einsum('bshn,rhn->bshr', q_nope, mla_w_uk)
    ckv = rmsnorm(zckv, mla_kv_norm)
    kpe = rope(zkr, pos)
    if past is None:
        k_a, v_a, ckv_all, kpe_all, k_pos = dk, dv, ckv, kpe, pos
    else:
        pk, pv, pckv, pkpe = past
        k_pos = jnp.concatenate([jnp.arange(pk.shape[1], dtype=jnp.int32), pos])
        k_a = jnp.concatenate([pk, dk], axis=1)
        v_a = jnp.concatenate([pv, dv], axis=1)
        ckv_all = jnp.concatenate([pckv, ckv], axis=1)
        kpe_all = jnp.concatenate([pkpe, kpe], axis=1)
    lambda_init = 0.8 - 0.6 * math.exp(-0.3 * layer_idx)
    lam_p = diff_lambda.astype(jnp.float32)
    lam = jnp.exp(jnp.sum(lam_p[0] * lam_p[1])) - jnp.exp(jnp.sum(lam_p[2] * lam_p[3])) + lambda_init
    o_a = over_query_blocks(lambda q, qp: diff_attention(q, k_a, v_a, qp, k_pos, lam, rel_bias), (dq,), pos)
    o_a = (rmsnorm(o_a, diff_subln) * (1.0 - lambda_init)).reshape(B, S, DA_WIDTH)
    o_lat = over_query_blocks(lambda ql, qe, qp: mla_attention(ql, qe, ckv_all, kpe_all, qp, k_pos), (q_lat, q_pe), pos)
    o_b = jnp.einsum('bshr,rhv->bshv', o_lat, mla_w_uv).reshape(B, S, MLA_WIDTH)
    merged = gates[:, :, 0] * (o_a @ w_branch_a) + gates[:, :, 1] * (o_b @ w_branch_b)
    x = x + merged @ w_out
    x = x + peer_ffn(rmsnorm(x, norm_ffn), peer_w_q, peer_keys, peer_u, peer_v)
    return x, (dk, dv, ckv, kpe)


def setup_inputs(seed: int = 0) -> dict:
    key = jax.random.key(seed)
    ks = jax.random.split(key, 32)

    def nrm(k, shape, scale):
        return jax.random.normal(k, shape, jnp.float32) * scale

    def gain(k, shape):
        return 1.0 + 0.02 * jax.random.normal(k, shape, jnp.float32)

    return {
        "x_prompt": nrm(ks[0], (BATCH, SEQ, D_MODEL), 1.0),
        "x_sample": nrm(ks[1], (DEC_BATCH, DEC_SEQ, D_MODEL), 1.0),
        "cache_diff_k": nrm(ks[2], (DEPTH, DEC_BATCH, PAST_LEN, DA_HEADS, 2, DA_HEAD_DIM), 1.0),
        "cache_diff_v": nrm(ks[3], (DEPTH, DEC_BATCH, PAST_LEN, DA_HEADS, 2 * DA_HEAD_DIM), 1.0),
        "cache_mla_ckv": nrm(ks[4], (DEPTH, DEC_BATCH, PAST_LEN, MLA_KV_LORA), 1.0),
        "cache_mla_kpe": nrm(ks[5], (DEPTH, DEC_BATCH, PAST_LEN, MLA_ROPE), 1.0),
        "rel_bias": nrm(ks[6], (REL_BUCKETS, DA_HEADS), 0.1),
        "norm_mix": gain(ks[7], (DEPTH, D_MODEL)),
        "w_in": nrm(ks[8], (DEPTH, D_MODEL, IN_WIDTH), D_MODEL ** -0.5),
        "diff_lambda": nrm(ks[9], (DEPTH, 4, DA_HEAD_DIM), 0.1),
        "diff_subln": gain(ks[10], (DEPTH, 2 * DA_HEAD_DIM)),
        "mla_q_norm": gain(ks[11], (DEPTH, MLA_Q_LORA)),
        "mla_w_uq": nrm(ks[12], (DEPTH, MLA_Q_LORA, MLA_HEADS, MLA_NOPE + MLA_ROPE), MLA_Q_LORA ** -0.5),
        "mla_kv_norm": gain(ks[13], (DEPTH, MLA_KV_LORA)),
        "mla_w_uk": nrm(ks[14], (DEPTH, MLA_KV_LORA, MLA_HEADS, MLA_NOPE), MLA_KV_LORA ** -0.5),
        "mla_w_uv": nrm(ks[15], (DEPTH, MLA_KV_LORA, MLA_HEADS, MLA_V), MLA_KV_LORA ** -0.5),
        "w_branch_a": nrm(ks[16], (DEPTH, DA_WIDTH, D_MODEL), DA_WIDTH ** -0.5),
        "w_branch_b": nrm(ks[17], (DEPTH, MLA_WIDTH, D_MODEL), MLA_WIDTH ** -0.5),
        "w_out": nrm(ks[18], (DEPTH, D_MODEL, D_MODEL), D_MODEL ** -0.5),
        "norm_ffn": gain(ks[19], (DEPTH, D_MODEL)),
        "peer_w_q": nrm(ks[20], (DEPTH, D_MODEL, PEER_HEADS * PEER_D_KEY), D_MODEL ** -0.5),
        "peer_keys": nrm(ks[21], (DEPTH, PEER_HEADS, 2, PEER_N_KEYS, PEER_D_KEY // 2), (PEER_D_KEY // 2) ** -0.5),
        "peer_u": nrm(ks[22], (DEPTH, PEER_N_EXPERTS, D_MODEL), D_MODEL ** -0.5),
        "peer_v": nrm(ks[23], (DEPTH, PEER_N_EXPERTS, D_MODEL), 0.1),
        "norm_final": gain(ks[24], (D_MODEL,)),
    }


def reference(x_prompt, x_sample, cache_diff_k, cache_diff_v, cache_mla_ckv, cache_mla_kpe,
              rel_bias, norm_mix, w_in, diff_lambda, diff_subln, mla_q_norm, mla_w_uq, mla_kv_norm,
              mla_w_uk, mla_w_uv, w_branch_a, w_branch_b, w_out, norm_ffn, peer_w_q, peer_keys,
              peer_u, peer_v, norm_final):
    pos_p = jnp.arange(x_prompt.shape[1], dtype=jnp.int32)
    pos_s = cache_diff_k.shape[2] + jnp.arange(x_sample.shape[1], dtype=jnp.int32)
    xp, xs = x_prompt, x_sample
    kp, vp, cp, ep = [], [], [], []
    ks_, vs_, cs_, es_ = [], [], [], []
    for l in range(DEPTH):
        w = (norm_mix[l], w_in[l], diff_lambda[l], diff_subln[l], mla_q_norm[l], mla_w_uq[l],
             mla_kv_norm[l], mla_w_uk[l], mla_w_uv[l], w_branch_a[l], w_branch_b[l], w_out[l],
             norm_ffn[l], peer_w_q[l], peer_keys[l], peer_u[l], peer_v[l])
        xp, (a, b, c, d) = trunk_layer(xp, pos_p, None, l, rel_bias, *w)
        kp.append(a); vp.append(b); cp.append(c); ep.append(d)
        past = (cache_diff_k[l], cache_diff_v[l], cache_mla_ckv[l], cache_mla_kpe[l])
        xs, (a, b, c, d) = trunk_layer(xs, pos_s, past, l, rel_bias, *w)
        ks_.append(a); vs_.append(b); cs_.append(c); es_.append(d)
    y_prompt = rmsnorm(xp, norm_final)
    y_sample = rmsnorm(xs, norm_final)
    return (y_prompt, y_sample,
            jnp.stack(kp), jnp.stack(vp), jnp.stack(cp), jnp.stack(ep),
            jnp.stack(ks_), jnp.stack(vs_), jnp.stack(cs_), jnp.stack(es_))
```

```python
import functools
import math

import jax
import jax.numpy as jnp
from jax import lax
from jax.experimental import pallas as pl
from jax.experimental.pallas import tpu as pltpu

F32 = jnp.float32
BF16 = jnp.bfloat16

CHUNK = 64
DA_HEADS = 8
DA_HEAD_DIM = 64
MLA_HEADS = 16
MLA_NOPE = 64
MLA_ROPE = 32
ROPE_THETA = 10000.0
REL_BUCKETS = 32
REL_MAX_DIST = 128
PEER_HEADS = 8
PEER_N_KEYS = 128
PEER_TOPK = 16
NORM_EPS = 1e-6
NEG_INF = -1e30

LANES = 128
PACK_ROWS = 4
TILE_STRIDE = 136
VMEM_LIMIT = 56 * 1024 * 1024


def _const_spec(shape):
    nd = len(shape)
    return pl.BlockSpec(shape, lambda *_: (0,) * nd, pipeline_mode=pl.Buffered(1))


def _rms(x, g):
    return x * lax.rsqrt(jnp.mean(x * x, axis=-1, keepdims=True) + NORM_EPS) * g


def _dot(a, b):
    return jnp.dot(a, b, preferred_element_type=F32)


def _dot_nt(a, b):
    return lax.dot_general(a, b, (((1,), (1,)), ((), ())), preferred_element_type=F32)


def _bmm_kernel(a_ref, b_ref, o_ref):
    o_ref[...] = jnp.dot(a_ref[...], b_ref[...], precision=lax.Precision.HIGHEST, preferred_element_type=F32)


def _bmm(a, b):
    n, m, k = a.shape
    _, _, p = b.shape
    return pl.pallas_call(
        _bmm_kernel,
        grid=(n,),
        in_specs=[pl.BlockSpec((None, m, k), lambda i: (i, 0, 0)), pl.BlockSpec((None, k, p), lambda i: (i, 0, 0))],
        out_specs=pl.BlockSpec((None, m, p), lambda i: (i, 0, 0)),
        out_shape=jax.ShapeDtypeStruct((n, m, p), F32),
        name="fold_bmm",
    )(a, b)


def _in_proj_kernel(x_ref, cos_ref, sin_ref, gmix_ref, wqkv_ref, wcq_ref, wkv_ref, wg_ref, gq_ref, gkv_ref, wqf_ref,
                    dq_ref, dk_ref, dv_ref, dkb_ref, dvb_ref, qm_ref, ckv_ref, kpe_ref, kf_ref, gate_ref, *, d, nh):
    hb = _rms(x_ref[...], gmix_ref[...]).astype(BF16)
    z = _dot(hb, wqkv_ref[...])
    dq_ref[...] = z[:, :d].astype(BF16)
    dk = z[:, d:2 * d]
    dk_ref[...] = dk
    dkb_ref[...] = dk.astype(BF16)
    dv = z[:, 2 * d:]
    dv_ref[...] = dv
    dvb_ref[...] = dv.astype(BF16)

    cosb = cos_ref[...]
    sinb = sin_ref[...]
    lane = lax.broadcasted_iota(jnp.int32, cosb.shape, 1)
    half = MLA_ROPE // 2

    def rope(blk):
        swapped = jnp.where(lane < half, pltpu.roll(blk, LANES - half, 1), pltpu.roll(blk, half, 1))
        return blk * cosb + swapped * sinb

    zkv = _dot(hb, wkv_ref[...])
    ckv = _rms(zkv[:, :LANES], gkv_ref[...])
    ckv_ref[...] = ckv
    kr = rope(zkv[:, LANES:])
    kpe_ref[...] = kr[:, :MLA_ROPE]
    kf_ref[...] = jnp.concatenate([ckv.astype(BF16), kr.astype(BF16)], axis=1)

    cq = _rms(_dot(hb, wcq_ref[...]), gq_ref[...]).astype(BF16)
    qraw = _dot(cq, wqf_ref[...])
    for h in range(nh):
        c0 = 2 * LANES * h
        qm_ref[:, c0:c0 + LANES] = qraw[:, c0:c0 + LANES].astype(BF16)
        qm_ref[:, c0 + LANES:c0 + 2 * LANES] = rope(qraw[:, c0 + LANES:c0 + 2 * LANES]).astype(BF16)

    zg = _dot(hb, wg_ref[...])
    gate_ref[...] = (1.0 / (1.0 + jnp.exp(-zg))).astype(BF16)


def _in_proj(x, cos_t, sin_t, gmix, wqkv, wcq, wkv, wg, gq, gkv, wqf, *, seq, tm):
    t, d = x.shape
    nh = wqf.shape[1] // (2 * LANES)
    nblk = seq // tm
    tok = lambda w: pl.BlockSpec((tm, w), lambda i: (i, 0))
    pos = pl.BlockSpec((tm, LANES), lambda i: (i % nblk, 0))
    outs = [
        (d, BF16), (d, F32), (d, F32), (d, BF16), (d, BF16), (wqf.shape[1], BF16), (LANES, F32), (MLA_ROPE, F32),
        (2 * LANES, BF16), (2 * d, BF16),
    ]
    return pl.pallas_call(
        functools.partial(_in_proj_kernel, d=d, nh=nh),
        grid=(t // tm,),
        in_specs=[tok(d), pos, pos, _const_spec(gmix.shape), _const_spec(wqkv.shape), _const_spec(wcq.shape),
                  _const_spec(wkv.shape), _const_spec(wg.shape), _const_spec(gq.shape), _const_spec(gkv.shape),
                  _const_spec(wqf.shape)],
        out_specs=[tok(w) for w, _ in outs],
        out_shape=[jax.ShapeDtypeStruct((t, w), dt) for w, dt in outs],
        compiler_params=pltpu.CompilerParams(dimension_semantics=("parallel",), vmem_limit_bytes=VMEM_LIMIT),
        name="in_proj",
    )(x, cos_t, sin_t, gmix, wqkv, wcq, wkv, wg, gq, gkv, wqf)


def _softmax_step(s, vb, m_ref, l_ref, acc_ref):
    m_old = m_ref[...]
    m_new = jnp.maximum(m_old, jnp.max(s, axis=-1, keepdims=True))
    alpha = jnp.exp(m_old - m_new)
    p = jnp.exp(s - m_new)
    l_ref[...] = alpha * l_ref[...] + jnp.sum(p, axis=-1, keepdims=True)
    acc_ref[...] = alpha * acc_ref[...] + _dot(p.astype(BF16), vb)
    m_ref[...] = m_new


def _init_state(m_ref, l_ref, acc_ref):
    m_ref[...] = jnp.full(m_ref.shape, -jnp.inf, F32)
    l_ref[...] = jnp.zeros(l_ref.shape, F32)
    acc_ref[...] = jnp.zeros(acc_ref.shape, F32)


def _diff_attn_kernel(cfar_ref, lam_ref, q_ref, k_ref, v_ref, bias_ref, g_ref, o_ref, m_ref, l_ref, acc_ref,
                      *, tq, tk, causal, out_scale):
    h = pl.program_id(1)
    i = pl.program_id(2)
    q = q_ref[...]
    lane = lax.broadcasted_iota(jnp.int32, q.shape, 1)
    zero = jnp.zeros_like(q)
    q2 = jnp.concatenate([jnp.where(lane < DA_HEAD_DIM, q, zero), jnp.where(lane >= DA_HEAD_DIM, q, zero)], axis=0)
    _init_state(m_ref, l_ref, acc_ref)

    def step(kb, vb, bias_tile=None, bias_scalar=None):
        s = _dot_nt(q2, kb)
        if bias_tile is not None:
            s = (s.reshape(2, tq, s.shape[-1]) + bias_tile[None]).reshape(s.shape)
        else:
            s = s + bias_scalar
        _softmax_step(s, vb, m_ref, l_ref, acc_ref)

    if causal:
        cfar = cfar_ref[h]

        def far(j, c):
            off = pl.multiple_of(j * tk, tk)
            step(k_ref[pl.ds(off, tk), :], v_ref[pl.ds(off, tk), :], bias_scalar=cfar)
            return c

        lax.fori_loop(0, jnp.maximum(i - 1, 0), far, 0)

        @pl.when(i > 0)
        def _():
            off = pl.multiple_of((i - 1) * tk, tk)
            step(k_ref[pl.ds(off, tk), :], v_ref[pl.ds(off, tk), :], bias_tile=bias_ref[0])

        off = pl.multiple_of(i * tk, tk)
        step(k_ref[pl.ds(off, tk), :], v_ref[pl.ds(off, tk), :], bias_tile=bias_ref[1])
    else:
        step(k_ref[...], v_ref[...], bias_tile=bias_ref[0])

    o = acc_ref[...] / l_ref[...]
    od = o[:tq] - lam_ref[0] * o[tq:]
    o_ref[...] = (_rms(od, g_ref[...]) * out_scale).astype(BF16)


def _diff_attn(q, k, v, bias, cfar, lam, g, *, tq, tk, causal, out_scale):
    b, sq, d = q.shape
    sk = k.shape[1]
    nh = d // LANES
    n_near = bias.shape[1]
    smem = pl.BlockSpec(memory_space=pltpu.SMEM)
    return pl.pallas_call(
        functools.partial(_diff_attn_kernel, tq=tq, tk=tk, causal=causal, out_scale=out_scale),
        grid=(b, nh, sq // tq),
        in_specs=[smem, smem,
                  pl.BlockSpec((None, tq, LANES), lambda bb, hh, ii: (bb, ii, hh)),
                  pl.BlockSpec((None, sk, LANES), lambda bb, hh, ii: (bb, 0, hh)),
                  pl.BlockSpec((None, sk, LANES), lambda bb, hh, ii: (bb, 0, hh)),
                  pl.BlockSpec((None, n_near, tq, bias.shape[3]), lambda bb, hh, ii: (hh, 0, 0, 0)),
                  pl.BlockSpec((1, LANES), lambda bb, hh, ii: (0, 0))],
        out_specs=pl.BlockSpec((None, tq, LANES), lambda bb, hh, ii: (bb, ii, hh)),
        out_shape=jax.ShapeDtypeStruct((b, sq, d), BF16),
        scratch_shapes=[pltpu.VMEM((2 * tq, 1), F32), pltpu.VMEM((2 * tq, 1), F32), pltpu.VMEM((2 * tq, LANES), F32)],
        compiler_params=pltpu.CompilerParams(dimension_semantics=("parallel", "parallel", "parallel"),
                                             vmem_limit_bytes=VMEM_LIMIT),
        name="diff_attn",
    )(cfar, lam, q, k, v, bias, g)


def _mla_attn_kernel(q_ref, kf_ref, mask_ref, o_ref, m_ref, l_ref, acc_ref, *, tq, tk, nh, causal):
    i = pl.program_id(1)
    q = jnp.concatenate([q_ref[:, 2 * LANES * h:2 * LANES * (h + 1)] for h in range(nh)], axis=0)
    _init_state(m_ref, l_ref, acc_ref)

    def step(kb, mask=None):
        s = _dot_nt(q, kb)
        if mask is not None:
            s = (s.reshape(nh, tq, s.shape[-1]) + mask[None]).reshape(s.shape)
        _softmax_step(s, kb[:, :LANES], m_ref, l_ref, acc_ref)

    if causal:
        def far(j, c):
            off = pl.multiple_of(j * tk, tk)
            step(kf_ref[pl.ds(off, tk), :])
            return c

        lax.fori_loop(0, i, far, 0)
        off = pl.multiple_of(i * tk, tk)
        step(kf_ref[pl.ds(off, tk), :], mask_ref[...])
    else:
        step(kf_ref[...])

    o = acc_ref[...] / l_ref[...]
    for h in range(nh):
        o_ref[:, LANES * h:LANES * (h + 1)] = o[h * tq:(h + 1) * tq].astype(BF16)


def _mla_attn(q, kf, mask, *, tq, tk, causal):
    b, sq, w = q.shape
    sk = kf.shape[1]
    nh = w // (2 * LANES)
    return pl.pallas_call(
        functools.partial(_mla_attn_kernel, tq=tq, tk=tk, nh=nh, causal=causal),
        grid=(b, sq // tq),
        in_specs=[pl.BlockSpec((None, tq, w), lambda bb, ii: (bb, ii, 0)),
                  pl.BlockSpec((None, sk, 2 * LANES), lambda bb, ii: (bb, 0, 0)),
                  pl.BlockSpec(mask.shape, lambda bb, ii: (0, 0))],
        out_specs=pl.BlockSpec((None, tq, nh * LANES), lambda bb, ii: (bb, ii, 0)),
        out_shape=jax.ShapeDtypeStruct((b, sq, nh * LANES), BF16),
        scratch_shapes=[pltpu.VMEM((nh * tq, 1), F32), pltpu.VMEM((nh * tq, 1), F32),
                        pltpu.VMEM((nh * tq, LANES), F32)],
        compiler_params=pltpu.CompilerParams(dimension_semantics=("parallel", "parallel"),
                                             vmem_limit_bytes=VMEM_LIMIT),
        name="mla_attn",
    )(q, kf, mask)


def _post_kernel(x_ref, oa_ref, ol_ref, gate_ref, wa_ref, wb_ref, wo_ref, gffn_ref, wst_ref, x2_ref, h2_ref, st_ref,
                 *, d):
    a = _dot(oa_ref[...], wa_ref[...])
    b = _dot(ol_ref[...], wb_ref[...])
    g = gate_ref[...].astype(F32)
    merged = g[:, :d] * a + g[:, d:] * b
    x2 = x_ref[...] + _dot(merged.astype(BF16), wo_ref[...])
    x2_ref[...] = x2
    h2 = _rms(x2, gffn_ref[...])
    h2_ref[...] = h2
    st_ref[...] = _dot_nt(wst_ref[...], h2.astype(BF16))


def _post(x, oa, ol, gate, wa, wb, wo, gffn, wst, *, tm):
    t, d = x.shape
    ns = wst.shape[0]
    tok = lambda w: pl.BlockSpec((tm, w), lambda i: (i, 0))
    return pl.pallas_call(
        functools.partial(_post_kernel, d=d),
        grid=(t // tm,),
        in_specs=[tok(d), tok(oa.shape[1]), tok(ol.shape[1]), tok(gate.shape[1]), _const_spec(wa.shape),
                  _const_spec(wb.shape), _const_spec(wo.shape), _const_spec(gffn.shape), _const_spec(wst.shape)],
        out_specs=[tok(d), tok(d), pl.BlockSpec((ns, tm), lambda i: (0, i))],
        out_shape=[jax.ShapeDtypeStruct((t, d), F32), jax.ShapeDtypeStruct((t, d), F32),
                   jax.ShapeDtypeStruct((ns, t), F32)],
        compiler_params=pltpu.CompilerParams(dimension_semantics=("parallel",), vmem_limit_bytes=VMEM_LIMIT),
        name="post",
    )(x, oa, ol, gate, wa, wb, wo, gffn, wst)


def _top_rows(s, payload=None):
    n = s.shape[0]
    iota = lax.broadcasted_iota(jnp.int32, s.shape, 0)
    vals, picks = [], []
    for _ in range(PEER_TOPK):
        m = jnp.max(s, axis=0, keepdims=True)
        pos = jnp.min(jnp.where(s == m, iota, n), axis=0, keepdims=True)
        sel = iota == pos
        vals.append(m)
        picks.append(pos if payload is None else jnp.sum(jnp.where(sel, payload, 0), axis=0, keepdims=True))
        s = jnp.where(sel, -jnp.inf, s)
    return jnp.concatenate(vals, axis=0), jnp.concatenate(picks, axis=0)


def _topk_kernel(st_ref, idx_ref, g_ref):
    nk = PEER_N_KEYS

    def head(h, c):
        base = pl.multiple_of(h * 2 * nk, 2 * nk)
        va, ia = _top_rows(st_ref[pl.ds(base, nk), :])
        vb, ib = _top_rows(st_ref[pl.ds(base + nk, nk), :])
        cand = jnp.concatenate([va[a:a + 1] + vb for a in range(PEER_TOPK)], axis=0)
        cidx = jnp.concatenate([ia[a:a + 1] * nk + ib for a in range(PEER_TOPK)], axis=0)
        tv, ti = _top_rows(cand, cidx)
        e = jnp.exp(tv - tv[0:1])
        ob = pl.multiple_of(h * PEER_TOPK, PEER_TOPK)
        idx_ref[pl.ds(ob, PEER_TOPK), :] = ti * PACK_ROWS
        g_ref[pl.ds(ob, PEER_TOPK), :] = e / jnp.sum(e, axis=0, keepdims=True)
        return c

    lax.fori_loop(0, PEER_HEADS, head, 0)


def _topk(st, *, tm):
    ns, t = st.shape
    nsel = PEER_HEADS * PEER_TOPK
    return pl.pallas_call(
        _topk_kernel,
        grid=(t // tm,),
        in_specs=[pl.BlockSpec((ns, tm), lambda i: (0, i))],
        out_specs=[pl.BlockSpec((nsel, tm), lambda i: (0, i)), pl.BlockSpec((nsel, tm), lambda i: (0, i))],
        out_shape=[jax.ShapeDtypeStruct((nsel, t), jnp.int32), jax.ShapeDtypeStruct((nsel, t), F32)],
        compiler_params=pltpu.CompilerParams(dimension_semantics=("parallel",), vmem_limit_bytes=VMEM_LIMIT),
        name="topk",
    )(st)


def _gather_rows(idx_ref, t, tab_ref, tile_ref, nsel):
    for mi in range(nsel):
        i = pl.multiple_of(idx_ref[t, mi], PACK_ROWS)
        tile_ref[pl.ds(mi, PACK_ROWS, stride=TILE_STRIDE), :] = tab_ref[pl.ds(i, PACK_ROWS), :]
    x = jnp.concatenate([tile_ref[q * TILE_STRIDE:q * TILE_STRIDE + nsel, :] for q in range(PACK_ROWS)], axis=1)
    return pltpu.bitcast(x, BF16)


def _two_rows(lo, hi):
    row = lax.broadcasted_iota(jnp.int32, (8, lo.shape[1]), 0)
    return jnp.where(row < 4, jnp.broadcast_to(lo, (8, lo.shape[1])), jnp.broadcast_to(hi, (8, hi.shape[1]))).astype(BF16)


def _peer_u_kernel(idx_ref, h_ref, g_ref, tab_ref, w_ref, tile_ref, z_ref, *, tm, nsel):
    half = h_ref.shape[1] // 2

    def tok(t, c):
        xb = _gather_rows(idx_ref, t, tab_ref, tile_ref, nsel)
        hrow = h_ref[pl.ds(t, 1), :]
        z8 = _dot_nt(_two_rows(hrow[:, :half], hrow[:, half:]), xb)
        lane = lax.broadcasted_iota(jnp.int32, (1, 2 * nsel), 1)
        z_ref[pl.ds(t, 1), :] = jnp.where(lane % 2 == 0, z8[0:1], z8[4:5])
        return c

    lax.fori_loop(0, tm, tok, 0)
    r = lax.broadcasted_iota(jnp.int32, (2 * nsel, nsel), 0)
    col = lax.broadcasted_iota(jnp.int32, (2 * nsel, nsel), 1)
    pair = (r // 2 == col).astype(F32)
    act = jnp.dot(z_ref[...], pair, precision=lax.Precision.HIGHEST, preferred_element_type=F32)
    gelu = 0.5 * act * (1.0 + lax.erf(act * (1.0 / math.sqrt(2.0))))
    w_ref[...] = g_ref[...] * gelu


def _peer_u(idx, h2, g, tab, *, tm):
    t, d = h2.shape
    nsel = idx.shape[1]
    tok = lambda w: pl.BlockSpec((tm, w), lambda i: (i, 0))
    return pl.pallas_call(
        functools.partial(_peer_u_kernel, tm=tm, nsel=nsel),
        grid=(t // tm,),
        in_specs=[pl.BlockSpec((tm, nsel), lambda i: (i, 0), memory_space=pltpu.SMEM), tok(d), tok(nsel),
                  _const_spec(tab.shape)],
        out_specs=tok(nsel),
        out_shape=jax.ShapeDtypeStruct((t, nsel), F32),
        scratch_shapes=[pltpu.VMEM((PACK_ROWS * TILE_STRIDE, LANES), jnp.int32), pltpu.VMEM((tm, 2 * nsel), F32)],
        compiler_params=pltpu.CompilerParams(dimension_semantics=("parallel",), vmem_limit_bytes=VMEM_LIMIT),
        name="peer_u",
    )(idx, h2, g, tab)


def _peer_v_kernel(idx_ref, w_ref, x2_ref, gfin_ref, tab_ref, y_ref, tile_ref, we_ref, wo_ref, po_ref, *, tm, nsel):
    wb = w_ref[...].astype(BF16)
    r = lax.broadcasted_iota(jnp.int32, (nsel, 2 * nsel), 0)
    col = lax.broadcasted_iota(jnp.int32, (nsel, 2 * nsel), 1)
    we_ref[...] = _dot(wb, (col == 2 * r).astype(BF16))
    wo_ref[...] = _dot(wb, (col == 2 * r + 1).astype(BF16))

    def tok(t, c):
        xb = _gather_rows(idx_ref, t, tab_ref, tile_ref, nsel)
        o8 = _dot(_two_rows(we_ref[pl.ds(t, 1), :], wo_ref[pl.ds(t, 1), :]), xb)
        po_ref[pl.ds(t, 1), :] = jnp.concatenate([o8[0:1], o8[4:5]], axis=1)
        return c

    lax.fori_loop(0, tm, tok, 0)
    y_ref[...] = _rms(x2_ref[...] + po_ref[...], gfin_ref[...])


def _peer_v(idx, w, x2, gfin, tab, *, tm):
    t, d = x2.shape
    nsel = idx.shape[1]
    tok = lambda wd: pl.BlockSpec((tm, wd), lambda i: (i, 0))
    return pl.pallas_call(
        functools.partial(_peer_v_kernel, tm=tm, nsel=nsel),
        grid=(t // tm,),
        in_specs=[pl.BlockSpec((tm, nsel), lambda i: (i, 0), memory_space=pltpu.SMEM), tok(nsel), tok(d),
                  _const_spec(gfin.shape), _const_spec(tab.shape)],
        out_specs=tok(d),
        out_shape=jax.ShapeDtypeStruct((t, d), F32),
        scratch_shapes=[pltpu.VMEM((PACK_ROWS * TILE_STRIDE, LANES), jnp.int32), pltpu.VMEM((tm, 2 * nsel), F32),
                        pltpu.VMEM((tm, 2 * nsel), F32), pltpu.VMEM((tm, d), F32)],
        compiler_params=pltpu.CompilerParams(dimension_semantics=("parallel",), vmem_limit_bytes=VMEM_LIMIT),
        name="peer_v",
    )(idx, w, x2, gfin, tab)


def _t5_bucket(rel):
    nb = REL_BUCKETS // 2
    max_exact = nb // 2
    ret = jnp.where(rel > 0, nb, 0)
    n = jnp.abs(rel)
    large = max_exact + (jnp.log(jnp.maximum(n, 1).astype(F32) / max_exact)
                         / math.log(REL_MAX_DIST / max_exact) * (nb - max_exact)).astype(jnp.int32)
    large = jnp.minimum(large, nb - 1)
    return ret + jnp.where(n < max_exact, n, large)


def _bias_tiles(rel_bias, q_pos, k_pos):
    bias = jnp.transpose(rel_bias[_t5_bucket(k_pos[None, :] - q_pos[:, None])], (2, 0, 1)).astype(F32)
    visible = (k_pos // CHUNK)[None, :] <= (q_pos // CHUNK)[:, None]
    return jnp.where(visible[None], bias, NEG_INF)


def _rope_tables(pos):
    half = MLA_ROPE // 2
    inv = ROPE_THETA ** (-jnp.arange(half, dtype=F32) / half)
    ang = pos.astype(F32)[:, None] * inv
    cos, sin = jnp.cos(ang), jnp.sin(ang)
    pad = jnp.zeros((pos.shape[0], LANES - MLA_ROPE), F32)
    return jnp.concatenate([cos, cos, pad], axis=1), jnp.concatenate([-sin, sin, pad], axis=1)


def _pack_table(tab):
    n, d = tab.shape
    bits = lax.bitcast_convert_type(tab.astype(BF16), jnp.uint16).astype(jnp.uint32)
    words = bits[:, :d // 2] | (bits[:, d // 2:] << 16)
    return lax.bitcast_convert_type(words, jnp.int32).reshape(n * PACK_ROWS, LANES)


def _layer_weights(l, norm_mix, w_in, diff_lambda, diff_subln, mla_q_norm, mla_w_uq, mla_kv_norm, mla_w_uk, mla_w_uv,
                   w_branch_a, w_branch_b, w_out, norm_ffn, peer_w_q, peer_keys, peer_u, peer_v):
    d = w_in.shape[1]
    da = DA_HEADS * 2 * DA_HEAD_DIM
    q_lora = mla_w_uq.shape[1]
    kv_lora = mla_w_uk.shape[1]
    nh = mla_w_uq.shape[2]
    win = w_in[l]
    c0 = 3 * da
    cuts = [c0, c0 + q_lora, c0 + q_lora + kv_lora, c0 + q_lora + kv_lora + MLA_ROPE]
    qscale = jnp.concatenate([jnp.full((da,), DA_HEAD_DIM ** -0.5, F32), jnp.ones((2 * da,), F32)])
    w = {}
    w["gmix"] = norm_mix[l][None]
    w["wqkv"] = (win[:, :c0] * qscale).astype(BF16)
    w["wcq"] = win[:, cuts[0]:cuts[1]].astype(BF16)
    w["wkv"] = jnp.concatenate([win[:, cuts[1]:cuts[3]], jnp.zeros((d, LANES - MLA_ROPE), F32)], axis=1).astype(BF16)
    w["wg"] = win[:, cuts[3]:].astype(BF16)
    w["gq"] = mla_q_norm[l][None]
    w["gkv"] = mla_kv_norm[l][None]
    uq = jnp.transpose(mla_w_uq[l], (1, 0, 2))
    uk = jnp.transpose(mla_w_uk[l], (1, 2, 0))
    wlat = _bmm(uq[:, :, :MLA_NOPE], uk)
    mla_scale = (MLA_NOPE + MLA_ROPE) ** -0.5
    wqf = jnp.concatenate([wlat, uq[:, :, MLA_NOPE:], jnp.zeros((nh, q_lora, LANES - MLA_ROPE), F32)], axis=2) * mla_scale
    w["wqf"] = jnp.transpose(wqf, (1, 0, 2)).reshape(q_lora, nh * 2 * LANES).astype(BF16)
    w["lam"] = (jnp.exp(jnp.sum(diff_lambda[l][0] * diff_lambda[l][1]))
                - jnp.exp(jnp.sum(diff_lambda[l][2] * diff_lambda[l][3]))).astype(F32)
    w["subln"] = diff_subln[l][None]
    w["wa"] = w_branch_a[l].astype(BF16)
    uv = jnp.transpose(mla_w_uv[l], (1, 0, 2))
    wbb = w_branch_b[l].reshape(nh, uv.shape[2], d)
    w["wb"] = _bmm(uv, wbb).reshape(nh * kv_lora, d).astype(BF16)
    w["wo"] = w_out[l].astype(BF16)
    w["gffn"] = norm_ffn[l][None]
    ng = PEER_HEADS * 2
    dk = peer_keys.shape[-1]
    keys = peer_keys[l].reshape(ng, PEER_N_KEYS, dk)
    wq_t = jnp.transpose(peer_w_q[l]).reshape(ng, dk, d)
    w["wst"] = _bmm(keys, wq_t).reshape(ng * PEER_N_KEYS, d).astype(BF16)
    w["utab"] = _pack_table(peer_u[l])
    w["vtab"] = _pack_table(peer_v[l])
    return w


def _group_forward(x, pos, past, layer_idx, rel_bias, w, gfin, *, tm, tm_post, tq_diff, tq_mla, tm_peer):
    b, s, d = x.shape
    t = b * s
    lambda_init = 0.8 - 0.6 * math.exp(-0.3 * layer_idx)
    cos_t, sin_t = _rope_tables(pos)
    xf = x.reshape(t, d)
    dq, dk, dv, dkb, dvb, qm, ckv, kpe, kf, gate = _in_proj(
        xf, cos_t, sin_t, w["gmix"], w["wqkv"], w["wcq"], w["wkv"], w["wg"], w["gq"], w["gkv"], w["wqf"], seq=s, tm=tm)
    lam = jnp.reshape(w["lam"] + lambda_init, (1,))
    nb = REL_BUCKETS // 2
    cfar = rel_bias[nb - 1].astype(F32)
    if past is None:
        assert s % tq_diff == 0 and s % tq_mla == 0 and tq_diff % CHUNK == 0 and tq_mla % CHUNK == 0
        assert tq_diff >= REL_MAX_DIST
        tq = tq_diff
        q_pos = jnp.arange(tq, 2 * tq, dtype=jnp.int32)
        k_pos = jnp.arange(0, 2 * tq, dtype=jnp.int32)
        both = _bias_tiles(rel_bias, q_pos, k_pos)
        bias = jnp.stack([both[:, :, :tq], both[:, :, tq:]], axis=1)
        oa = _diff_attn(dq.reshape(b, s, d), dkb.reshape(b, s, d), dvb.reshape(b, s, d), bias, cfar, lam, w["subln"],
                        tq=tq, tk=tq, causal=True, out_scale=1.0 - lambda_init)
        tqm = tq_mla
        lp = jnp.arange(tqm, dtype=jnp.int32)
        mask = jnp.where((lp // CHUNK)[None, :] <= (lp // CHUNK)[:, None], 0.0, NEG_INF).astype(F32)
        ol = _mla_attn(qm.reshape(b, s, -1), kf.reshape(b, s, -1), mask, tq=tqm, tk=tqm, causal=True)
    else:
        pk, pv, pckv, pkpe = past
        p = pk.shape[1]
        k_pos = jnp.concatenate([jnp.arange(p, dtype=jnp.int32), pos])
        k_all = jnp.concatenate([pk.reshape(b, p, d).astype(BF16), dkb.reshape(b, s, d)], axis=1)
        v_all = jnp.concatenate([pv.reshape(b, p, d).astype(BF16), dvb.reshape(b, s, d)], axis=1)
        bias = _bias_tiles(rel_bias, pos, k_pos)[:, None]
        oa = _diff_attn(dq.reshape(b, s, d), k_all, v_all, bias, cfar, lam, w["subln"],
                        tq=s, tk=p + s, causal=False, out_scale=1.0 - lambda_init)
        pad = jnp.zeros((b, p, LANES - MLA_ROPE), BF16)
        kf_past = jnp.concatenate([pckv.astype(BF16), pkpe.astype(BF16), pad], axis=2)
        kf_all = jnp.concatenate([kf_past, kf.reshape(b, s, -1)], axis=1)
        mask = jnp.zeros((8, LANES), F32)
        ol = _mla_attn(qm.reshape(b, s, -1), kf_all, mask, tq=s, tk=p + s, causal=False)
    x2, h2, st = _post(xf, oa.reshape(t, d), ol.reshape(t, -1), gate, w["wa"], w["wb"], w["wo"], w["gffn"], w["wst"],
                       tm=tm_post)
    idx_t, g_t = _topk(st, tm=tm_peer)
    idx = jnp.transpose(idx_t)
    wts = _peer_u(idx, h2, jnp.transpose(g_t), w["utab"], tm=tm_peer)
    y = _peer_v(idx, wts, x2, gfin, w["vtab"], tm=tm_peer)
    return y.reshape(b, s, d), (dk, dv, ckv, kpe)


def kernel(x_prompt, x_sample, cache_diff_k, cache_diff_v, cache_mla_ckv, cache_mla_kpe, rel_bias, norm_mix, w_in,
           diff_lambda, diff_subln, mla_q_norm, mla_w_uq, mla_kv_norm, mla_w_uk, mla_w_uv, w_branch_a, w_branch_b,
           w_out, norm_ffn, peer_w_q, peer_keys, peer_u, peer_v, norm_final):
    depth = w_in.shape[0]
    assert depth == 1, "the final RMSNorm is fused into the last layer's PEER kernel"
    bp, sp, d = x_prompt.shape
    bs, ss, _ = x_sample.shape
    past_len = cache_diff_k.shape[2]
    pos_p = jnp.arange(sp, dtype=jnp.int32)
    pos_s = past_len + jnp.arange(ss, dtype=jnp.int32)
    gfin = norm_final[None]
    l = 0
    w = _layer_weights(l, norm_mix, w_in, diff_lambda, diff_subln, mla_q_norm, mla_w_uq, mla_kv_norm, mla_w_uk,
                       mla_w_uv, w_branch_a, w_branch_b, w_out, norm_ffn, peer_w_q, peer_keys, peer_u, peer_v)
    yp, (kp, vp, cp, ep) = _group_forward(x_prompt, pos_p, None, l, rel_bias, w, gfin,
                                          tm=min(256, sp), tm_post=256, tq_diff=min(256, sp), tq_mla=min(128, sp),
                                          tm_peer=128)
    past = (cache_diff_k[l], cache_diff_v[l], cache_mla_ckv[l], cache_mla_kpe[l])
    ys, (ks, vs, cs, es) = _group_forward(x_sample, pos_s, past, l, rel_bias, w, gfin,
                                          tm=ss, tm_post=128, tq_diff=ss, tq_mla=ss, tm_peer=128)
    nh, hd = DA_HEADS, DA_HEAD_DIM
    return (yp, ys,
            kp.reshape(1, bp, sp, nh, 2, hd), vp.reshape(1, bp, sp, nh, 2 * hd),
            cp.reshape(1, bp, sp, -1), ep.reshape(1, bp, sp, -1),
            ks.reshape(1, bs, ss, nh, 2, hd), vs.reshape(1, bs, ss, nh, 2 * hd),
            cs.reshape(1, bs, ss, -1), es.reshape(1, bs, ss, -1))
```

```python
import functools
import math

import jax
import jax.numpy as jnp
from jax import lax
from jax.experimental import pallas as pl
from jax.experimental.pallas import tpu as pltpu

F32 = jnp.float32
BF16 = jnp.bfloat16

CHUNK = 64
DA_HEADS = 8
DA_HEAD_DIM = 64
MLA_HEADS = 16
MLA_NOPE = 64
MLA_ROPE = 32
ROPE_THETA = 10000.0
REL_BUCKETS = 32
REL_MAX_DIST = 128
PEER_HEADS = 8
PEER_N_KEYS = 128
PEER_TOPK = 16
NORM_EPS = 1e-6
NEG_INF = -1e30
LOG2E = math.log2(math.e)

LANES = 128
PACK_ROWS = 4
TILE_STRIDE = 136
VMEM_LIMIT = 56 * 1024 * 1024
GROUPS_PER_ITER = 2


def _const_spec(shape):
    nd = len(shape)
    return pl.BlockSpec(shape, lambda *_: (0,) * nd, pipeline_mode=pl.Buffered(1))


def _rms(x, g):
    return x * lax.rsqrt(jnp.mean(x * x, axis=-1, keepdims=True) + NORM_EPS) * g


def _dot(a, b):
    return jnp.dot(a, b, preferred_element_type=F32)


def _dot_nt(a, b):
    return lax.dot_general(a, b, (((1,), (1,)), ((), ())), preferred_element_type=F32)


def _bmm_kernel(a_ref, b_ref, o_ref):
    o_ref[...] = jnp.dot(a_ref[...], b_ref[...], precision=lax.Precision.HIGHEST, preferred_element_type=F32)


def _bmm(a, b):
    n, m, k = a.shape
    _, _, p = b.shape
    return pl.pallas_call(
        _bmm_kernel,
        grid=(n,),
        in_specs=[pl.BlockSpec((None, m, k), lambda i: (i, 0, 0)), pl.BlockSpec((None, k, p), lambda i: (i, 0, 0))],
        out_specs=pl.BlockSpec((None, m, p), lambda i: (i, 0, 0)),
        out_shape=jax.ShapeDtypeStruct((n, m, p), F32),
        name="fold_bmm",
    )(a, b)


def _in_proj_kernel(x_ref, cos_ref, sin_ref, gmix_ref, wqkv_ref, wcq_ref, wkv_ref, wg_ref, gq_ref, gkv_ref, wqf_ref,
                    dq_ref, dk_ref, dv_ref, dkb_ref, dvb_ref, qm_ref, ckv_ref, kpe_ref, kf_ref, gate_ref, *, d, nh):
    hb = _rms(x_ref[...], gmix_ref[...]).astype(BF16)
    z = _dot(hb, wqkv_ref[...])
    dq_ref[...] = z[:, :d].astype(BF16)
    dk = z[:, d:2 * d]
    dk_ref[...] = dk
    dkb_ref[...] = dk.astype(BF16)
    dv = z[:, 2 * d:]
    dv_ref[...] = dv
    dvb_ref[...] = dv.astype(BF16)

    cosb = cos_ref[...]
    sinb = sin_ref[...]
    lane = lax.broadcasted_iota(jnp.int32, cosb.shape, 1)
    half = MLA_ROPE // 2

    def rope(blk):
        swapped = jnp.where(lane < half, pltpu.roll(blk, LANES - half, 1), pltpu.roll(blk, half, 1))
        return blk * cosb + swapped * sinb

    zkv = _dot(hb, wkv_ref[...])
    ckv = _rms(zkv[:, :LANES], gkv_ref[...])
    ckv_ref[...] = ckv
    kr = rope(zkv[:, LANES:])
    kpe_ref[...] = kr[:, :MLA_ROPE]
    kf_ref[...] = jnp.concatenate([ckv.astype(BF16), kr.astype(BF16)], axis=1)

    cq = _rms(_dot(hb, wcq_ref[...]), gq_ref[...]).astype(BF16)
    qraw = _dot(cq, wqf_ref[...])
    for h in range(nh):
        c0 = 2 * LANES * h
        qm_ref[:, c0:c0 + LANES] = qraw[:, c0:c0 + LANES].astype(BF16)
        qm_ref[:, c0 + LANES:c0 + 2 * LANES] = rope(qraw[:, c0 + LANES:c0 + 2 * LANES]).astype(BF16)

    zg = _dot(hb, wg_ref[...])
    gate_ref[...] = (1.0 / (1.0 + jnp.exp(-zg))).astype(BF16)


def _in_proj(x, cos_t, sin_t, gmix, wqkv, wcq, wkv, wg, gq, gkv, wqf, *, seq, tm):
    t, d = x.shape
    nh = wqf.shape[1] // (2 * LANES)
    nblk = seq // tm
    tok = lambda w: pl.BlockSpec((tm, w), lambda i: (i, 0))
    pos = pl.BlockSpec((tm, LANES), lambda i: (i % nblk, 0))
    outs = [
        (d, BF16), (d, F32), (d, F32), (d, BF16), (d, BF16), (wqf.shape[1], BF16), (LANES, F32), (MLA_ROPE, F32),
        (2 * LANES, BF16), (2 * d, BF16),
    ]
    return pl.pallas_call(
        functools.partial(_in_proj_kernel, d=d, nh=nh),
        grid=(t // tm,),
        in_specs=[tok(d), pos, pos, _const_spec(gmix.shape), _const_spec(wqkv.shape), _const_spec(wcq.shape),
                  _const_spec(wkv.shape), _const_spec(wg.shape), _const_spec(gq.shape), _const_spec(gkv.shape),
                  _const_spec(wqf.shape)],
        out_specs=[tok(w) for w, _ in outs],
        out_shape=[jax.ShapeDtypeStruct((t, w), dt) for w, dt in outs],
        compiler_params=pltpu.CompilerParams(dimension_semantics=("parallel",), vmem_limit_bytes=VMEM_LIMIT),
        name="in_proj",
    )(x, cos_t, sin_t, gmix, wqkv, wcq, wkv, wg, gq, gkv, wqf)


def _attn_step(q_all, kb, vb, bias_fn, state, *, rs):
    s_ref, p_ref, m_ref, l_ref, a_ref, acc_ref = state
    rows, tk = s_ref.shape
    nrep = tk // LANES
    s_ref[...] = _dot_nt(q_all, kb)

    def groups(r, c):
        for u in range(GROUPS_PER_ITER):
            r0 = pl.multiple_of((r * GROUPS_PER_ITER + u) * rs, rs)
            s = s_ref[pl.ds(r0, rs), :]
            if bias_fn is not None:
                s = bias_fn(s, r0)
            m_old = m_ref[pl.ds(r0, rs), :]
            m_new = jnp.maximum(m_old, jnp.max(s, axis=-1, keepdims=True))
            alpha = jnp.exp2(m_old - m_new)
            p = jnp.exp2(s - pltpu.repeat(m_new, nrep, axis=1))
            psum = p[:, :LANES]
            for j in range(1, nrep):
                psum = psum + p[:, j * LANES:(j + 1) * LANES]
            l_ref[pl.ds(r0, rs), :] = alpha * l_ref[pl.ds(r0, rs), :] + psum
            m_ref[pl.ds(r0, rs), :] = m_new
            a_ref[pl.ds(r0, rs), :] = alpha
            p_ref[pl.ds(r0, rs), :] = p.astype(BF16)
        return c

    assert rows % (rs * GROUPS_PER_ITER) == 0
    lax.fori_loop(0, rows // (rs * GROUPS_PER_ITER), groups, 0)
    acc_ref[...] = a_ref[...] * acc_ref[...] + _dot(p_ref[...], vb)


def _init_state(state):
    _, _, m_ref, l_ref, _, acc_ref = state
    m_ref[...] = jnp.full(m_ref.shape, -jnp.inf, F32)
    l_ref[...] = jnp.zeros(l_ref.shape, F32)
    acc_ref[...] = jnp.zeros(acc_ref.shape, F32)


def _attn_state_shapes(rows, tk):
    return [pltpu.VMEM((rows, tk), F32), pltpu.VMEM((rows, tk), BF16), pltpu.VMEM((rows, LANES), F32),
            pltpu.VMEM((rows, LANES), F32), pltpu.VMEM((rows, LANES), F32), pltpu.VMEM((rows, LANES), F32)]


def _attn_finish(state):
    _, _, _, l_ref, _, acc_ref = state
    return acc_ref[...] / jnp.sum(l_ref[...], axis=-1, keepdims=True)


def _diff_attn_kernel(cfar_ref, lam_ref, q_ref, k_ref, v_ref, bias_ref, g_ref, o_ref, q2_ref, *state,
                      tq, tk, rs, causal, out_scale):
    h = pl.program_id(1)
    i = pl.program_id(2)
    q = q_ref[...]
    lane = lax.broadcasted_iota(jnp.int32, q.shape, 1)
    zero = jnp.zeros_like(q)
    q2_ref[:tq] = jnp.where(lane < DA_HEAD_DIM, q, zero)
    q2_ref[tq:] = jnp.where(lane >= DA_HEAD_DIM, q, zero)
    _init_state(state)

    def tile_bias(n):
        def fn(s, r0):
            b0 = pl.multiple_of(r0 - jnp.where(r0 >= tq, tq, 0), rs)
            return s + bias_ref[n, pl.ds(b0, rs), :]
        return fn

    def step(off, bias_fn):
        _attn_step(q2_ref[...], k_ref[pl.ds(off, tk), :], v_ref[pl.ds(off, tk), :], bias_fn, state, rs=rs)

    if causal:
        cfar = cfar_ref[h]

        def far(j, c):
            step(pl.multiple_of(j * tk, tk), lambda s, r0: s + cfar)
            return c

        lax.fori_loop(0, jnp.maximum(i - 1, 0), far, 0)

        @pl.when(i > 0)
        def _():
            step(pl.multiple_of((i - 1) * tk, tk), tile_bias(0))

        step(pl.multiple_of(i * tk, tk), tile_bias(1))
    else:
        step(0, tile_bias(0))

    o = _attn_finish(state)
    od = o[:tq] - lam_ref[0] * o[tq:]
    o_ref[...] = (_rms(od, g_ref[...]) * out_scale).astype(BF16)


def _diff_attn(q, k, v, bias, cfar, lam, g, *, tq, tk, rs, causal, out_scale):
    b, sq, d = q.shape
    sk = k.shape[1]
    nh = d // LANES
    n_near = bias.shape[1]
    assert tq % rs == 0 and tk % LANES == 0 and sk % tk == 0
    smem = pl.BlockSpec(memory_space=pltpu.SMEM)
    return pl.pallas_call(
        functools.partial(_diff_attn_kernel, tq=tq, tk=tk, rs=rs, causal=causal, out_scale=out_scale),
        grid=(b, nh, sq // tq),
        in_specs=[smem, smem,
                  pl.BlockSpec((None, tq, LANES), lambda bb, hh, ii: (bb, ii, hh)),
                  pl.BlockSpec((None, sk, LANES), lambda bb, hh, ii: (bb, 0, hh)),
                  pl.BlockSpec((None, sk, LANES), lambda bb, hh, ii: (bb, 0, hh)),
                  pl.BlockSpec((None, n_near, tq, bias.shape[3]), lambda bb, hh, ii: (hh, 0, 0, 0)),
                  pl.BlockSpec((1, LANES), lambda bb, hh, ii: (0, 0))],
        out_specs=pl.BlockSpec((None, tq, LANES), lambda bb, hh, ii: (bb, ii, hh)),
        out_shape=jax.ShapeDtypeStruct((b, sq, d), BF16),
        scratch_shapes=[pltpu.VMEM((2 * tq, LANES), BF16)] + _attn_state_shapes(2 * tq, tk),
        compiler_params=pltpu.CompilerParams(dimension_semantics=("parallel", "parallel", "parallel"),
                                             vmem_limit_bytes=VMEM_LIMIT),
        name="diff_attn",
    )(cfar, lam, q, k, v, bias, g)


def _mla_attn_kernel(q_ref, kf_ref, mask_ref, o_ref, qs_ref, *state, tq, tk, rs, nh, causal):
    i = pl.program_id(2)
    for h in range(nh):
        qs_ref[h * tq:(h + 1) * tq, :] = q_ref[:, 2 * LANES * h:2 * LANES * (h + 1)]
    _init_state(state)

    def mask_fn(s, r0):
        b0 = pl.multiple_of(lax.rem(r0, tq), rs)
        return s + mask_ref[pl.ds(b0, rs), :]

    def step(off, bias_fn):
        _attn_step(qs_ref[...], kf_ref[pl.ds(off, tk), :], kf_ref[pl.ds(off, tk), :LANES], bias_fn, state, rs=rs)

    if causal:
        def far(j, c):
            step(pl.multiple_of(j * tk, tk), None)
            return c

        lax.fori_loop(0, i, far, 0)
        step(pl.multiple_of(i * tk, tk), mask_fn)
    else:
        step(0, mask_fn)

    o = _attn_finish(state)
    for h in range(nh):
        o_ref[:, LANES * h:LANES * (h + 1)] = o[h * tq:(h + 1) * tq].astype(BF16)


def _mla_attn(q, kf, mask, *, tq, tk, rs, nh, causal):
    b, sq, w = q.shape
    sk = kf.shape[1]
    ngrp = w // (2 * LANES * nh)
    assert tq % rs == 0 and tk % LANES == 0 and sk % tk == 0 and mask.shape == (tq, tk)
    return pl.pallas_call(
        functools.partial(_mla_attn_kernel, tq=tq, tk=tk, rs=rs, nh=nh, causal=causal),
        grid=(b, ngrp, sq // tq),
        in_specs=[pl.BlockSpec((None, tq, nh * 2 * LANES), lambda bb, gg, ii: (bb, ii, gg)),
                  pl.BlockSpec((None, sk, 2 * LANES), lambda bb, gg, ii: (bb, 0, 0)),
                  pl.BlockSpec(mask.shape, lambda bb, gg, ii: (0, 0))],
        out_specs=pl.BlockSpec((None, tq, nh * LANES), lambda bb, gg, ii: (bb, ii, gg)),
        out_shape=jax.ShapeDtypeStruct((b, sq, ngrp * nh * LANES), BF16),
        scratch_shapes=[pltpu.VMEM((nh * tq, 2 * LANES), BF16)] + _attn_state_shapes(nh * tq, tk),
        compiler_params=pltpu.CompilerParams(dimension_semantics=("parallel", "parallel", "parallel"),
                                             vmem_limit_bytes=VMEM_LIMIT),
        name="mla_attn",
    )(q, kf, mask)


def _post_kernel(x_ref, oa_ref, ol_ref, gate_ref, wa_ref, wb_ref, wo_ref, gffn_ref, wst_ref, x2_ref, h2_ref, st_ref,
                 *, d):
    a = _dot(oa_ref[...], wa_ref[...])
    b = _dot(ol_ref[...], wb_ref[...])
    g = gate_ref[...].astype(F32)
    merged = g[:, :d] * a + g[:, d:] * b
    x2 = x_ref[...] + _dot(merged.astype(BF16), wo_ref[...])
    x2_ref[...] = x2
    h2 = _rms(x2, gffn_ref[...])
    h2_ref[...] = h2
    st_ref[...] = _dot_nt(wst_ref[...], h2.astype(BF16))


def _post(x, oa, ol, gate, wa, wb, wo, gffn, wst, *, tm):
    t, d = x.shape
    ns = wst.shape[0]
    tok = lambda w: pl.BlockSpec((tm, w), lambda i: (i, 0))
    return pl.pallas_call(
        functools.partial(_post_kernel, d=d),
        grid=(t // tm,),
        in_specs=[tok(d), tok(oa.shape[1]), tok(ol.shape[1]), tok(gate.shape[1]), _const_spec(wa.shape),
                  _const_spec(wb.shape), _const_spec(wo.shape), _const_spec(gffn.shape), _const_spec(wst.shape)],
        out_specs=[tok(d), tok(d), pl.BlockSpec((ns, tm), lambda i: (0, i))],
        out_shape=[jax.ShapeDtypeStruct((t, d), F32), jax.ShapeDtypeStruct((t, d), F32),
                   jax.ShapeDtypeStruct((ns, t), F32)],
        compiler_params=pltpu.CompilerParams(dimension_semantics=("parallel",), vmem_limit_bytes=VMEM_LIMIT),
        name="post",
    )(x, oa, ol, gate, wa, wb, wo, gffn, wst)


def _top_rows(s, order=None, payload=None):
    if order is None:
        order = lax.broadcasted_iota(jnp.int32, s.shape, 0)
    big = jnp.int32(2 ** 30)
    vals, picks = [], []
    for _ in range(PEER_TOPK):
        m = jnp.max(s, axis=0, keepdims=True)
        pos = jnp.min(jnp.where(s == m, order, big), axis=0, keepdims=True)
        sel = order == pos
        vals.append(m)
        picks.append(pos if payload is None else jnp.sum(jnp.where(sel, payload, 0), axis=0, keepdims=True))
        s = jnp.where(sel, -jnp.inf, s)
    return jnp.concatenate(vals, axis=0), jnp.concatenate(picks, axis=0)


def _pair_candidates(va, ia, vb, ib):
    k, nk = PEER_TOPK, PEER_N_KEYS
    split = 4
    unused = jnp.int32(2 ** 29)
    vals, orders, idxs = [], [], []
    for a in range(split):
        n = k // (a + 1)
        rows = -(-n // 8) * 8
        r = lax.broadcasted_iota(jnp.int32, (rows, va.shape[1]), 0)
        valid = r < n
        vals.append(jnp.where(valid, va[a:a + 1] + vb[:rows], -jnp.inf))
        orders.append(jnp.where(valid, a * k + r, unused))
        idxs.append(ia[a:a + 1] * nk + ib[:rows])
    for b in range(k // (split + 1)):
        n = k // (b + 1)
        rows = -(-n // 8) * 8
        r = lax.broadcasted_iota(jnp.int32, (rows, va.shape[1]), 0)
        valid = (r >= split) & (r < n)
        vals.append(jnp.where(valid, va[:rows] + vb[b:b + 1], -jnp.inf))
        orders.append(jnp.where(valid, r * k + b, unused))
        idxs.append(ia[:rows] * nk + ib[b:b + 1])
    return jnp.concatenate(vals, axis=0), jnp.concatenate(orders, axis=0), jnp.concatenate(idxs, axis=0)


def _topk_kernel(st_ref, idx_ref, g_ref):
    nk = PEER_N_KEYS

    def head(h, c):
        base = pl.multiple_of(h * 2 * nk, 2 * nk)
        va, ia = _top_rows(st_ref[pl.ds(base, nk), :])
        vb, ib = _top_rows(st_ref[pl.ds(base + nk, nk), :])
        cand, order, cidx = _pair_candidates(va, ia, vb, ib)
        tv, ti = _top_rows(cand, order, cidx)
        e = jnp.exp(tv - tv[0:1])
        ob = pl.multiple_of(h * PEER_TOPK, PEER_TOPK)
        idx_ref[pl.ds(ob, PEER_TOPK), :] = ti * PACK_ROWS
        g_ref[pl.ds(ob, PEER_TOPK), :] = e / jnp.sum(e, axis=0, keepdims=True)
        return c

    lax.fori_loop(0, PEER_HEADS, head, 0)


def _topk(st, *, tm):
    ns, t = st.shape
    nsel = PEER_HEADS * PEER_TOPK
    return pl.pallas_call(
        _topk_kernel,
        grid=(t // tm,),
        in_specs=[pl.BlockSpec((ns, tm), lambda i: (0, i))],
        out_specs=[pl.BlockSpec((nsel, tm), lambda i: (0, i)), pl.BlockSpec((nsel, tm), lambda i: (0, i))],
        out_shape=[jax.ShapeDtypeStruct((nsel, t), jnp.int32), jax.ShapeDtypeStruct((nsel, t), F32)],
        compiler_params=pltpu.CompilerParams(dimension_semantics=("parallel",), vmem_limit_bytes=VMEM_LIMIT),
        name="topk",
    )(st)


def _gather_rows(idx_ref, t, tab_ref, tile_ref, nsel):
    for mi in range(nsel):
        i = pl.multiple_of(idx_ref[t, mi], PACK_ROWS)
        tile_ref[pl.ds(mi, PACK_ROWS, stride=TILE_STRIDE), :] = tab_ref[pl.ds(i, PACK_ROWS), :]


def _tile_rows(tile_ref, nsel):
    x = jnp.concatenate([tile_ref[q * TILE_STRIDE:q * TILE_STRIDE + nsel, :] for q in range(PACK_ROWS)], axis=1)
    return pltpu.bitcast(x, BF16)


def _pipelined_tokens(tm, gather, compute, tile_a, tile_b):
    assert tm % 2 == 0
    gather(0, tile_a)

    def pair(i, c):
        t0 = 2 * i
        gather(t0 + 1, tile_b)
        compute(t0, tile_a)
        gather(jnp.minimum(t0 + 2, tm - 1), tile_a)
        compute(t0 + 1, tile_b)
        return c

    lax.fori_loop(0, tm // 2, pair, 0)


def _two_rows(lo, hi):
    row = lax.broadcasted_iota(jnp.int32, (8, lo.shape[1]), 0)
    return jnp.where(row < 4, jnp.broadcast_to(lo, (8, lo.shape[1])), jnp.broadcast_to(hi, (8, hi.shape[1]))).astype(BF16)


def _peer_u_kernel(idx_ref, h_ref, g_ref, tab_ref, w_ref, tile_a, tile_b, z_ref, *, tm, nsel):
    half = h_ref.shape[1] // 2

    def gather(t, tile_ref):
        _gather_rows(idx_ref, t, tab_ref, tile_ref, nsel)

    def compute(t, tile_ref):
        hrow = h_ref[pl.ds(t, 1), :]
        z8 = _dot_nt(_two_rows(hrow[:, :half], hrow[:, half:]), _tile_rows(tile_ref, nsel))
        lane = lax.broadcasted_iota(jnp.int32, (1, 2 * nsel), 1)
        z_ref[pl.ds(t, 1), :] = jnp.where(lane % 2 == 0, z8[0:1], z8[4:5])

    _pipelined_tokens(tm, gather, compute, tile_a, tile_b)
    r = lax.broadcasted_iota(jnp.int32, (2 * nsel, nsel), 0)
    col = lax.broadcasted_iota(jnp.int32, (2 * nsel, nsel), 1)
    pair = (r // 2 == col).astype(F32)
    act = jnp.dot(z_ref[...], pair, precision=lax.Precision.HIGHEST, preferred_element_type=F32)
    gelu = 0.5 * act * (1.0 + lax.erf(act * (1.0 / math.sqrt(2.0))))
    w_ref[...] = g_ref[...] * gelu


def _peer_u(idx, h2, g, tab, *, tm):
    t, d = h2.shape
    nsel = idx.shape[1]
    tok = lambda w: pl.BlockSpec((tm, w), lambda i: (i, 0))
    return pl.pallas_call(
        functools.partial(_peer_u_kernel, tm=tm, nsel=nsel),
        grid=(t // tm,),
        in_specs=[pl.BlockSpec((tm, nsel), lambda i: (i, 0), memory_space=pltpu.SMEM), tok(d), tok(nsel),
                  _const_spec(tab.shape)],
        out_specs=tok(nsel),
        out_shape=jax.ShapeDtypeStruct((t, nsel), F32),
        scratch_shapes=[pltpu.VMEM((PACK_ROWS * TILE_STRIDE, LANES), jnp.int32),
                        pltpu.VMEM((PACK_ROWS * TILE_STRIDE, LANES), jnp.int32), pltpu.VMEM((tm, 2 * nsel), F32)],
        compiler_params=pltpu.CompilerParams(dimension_semantics=("parallel",), vmem_limit_bytes=VMEM_LIMIT),
        name="peer_u",
    )(idx, h2, g, tab)


def _peer_v_kernel(idx_ref, w_ref, x2_ref, gfin_ref, tab_ref, y_ref, tile_a, tile_b, we_ref, wo_ref, po_ref,
                   *, tm, nsel):
    wb = w_ref[...].astype(BF16)
    r = lax.broadcasted_iota(jnp.int32, (nsel, 2 * nsel), 0)
    col = lax.broadcasted_iota(jnp.int32, (nsel, 2 * nsel), 1)
    we_ref[...] = _dot(wb, (col == 2 * r).astype(BF16))
    wo_ref[...] = _dot(wb, (col == 2 * r + 1).astype(BF16))

    def gather(t, tile_ref):
        _gather_rows(idx_ref, t, tab_ref, tile_ref, nsel)

    def compute(t, tile_ref):
        o8 = _dot(_two_rows(we_ref[pl.ds(t, 1), :], wo_ref[pl.ds(t, 1), :]), _tile_rows(tile_ref, nsel))
        po_ref[pl.ds(t, 1), :] = jnp.concatenate([o8[0:1], o8[4:5]], axis=1)

    _pipelined_tokens(tm, gather, compute, tile_a, tile_b)
    y_ref[...] = _rms(x2_ref[...] + po_ref[...], gfin_ref[...])


def _peer_v(idx, w, x2, gfin, tab, *, tm):
    t, d = x2.shape
    nsel = idx.shape[1]
    tok = lambda wd: pl.BlockSpec((tm, wd), lambda i: (i, 0))
    return pl.pallas_call(
        functools.partial(_peer_v_kernel, tm=tm, nsel=nsel),
        grid=(t // tm,),
        in_specs=[pl.BlockSpec((tm, nsel), lambda i: (i, 0), memory_space=pltpu.SMEM), tok(nsel), tok(d),
                  _const_spec(gfin.shape), _const_spec(tab.shape)],
        out_specs=tok(d),
        out_shape=jax.ShapeDtypeStruct((t, d), F32),
        scratch_shapes=[pltpu.VMEM((PACK_ROWS * TILE_STRIDE, LANES), jnp.int32),
                        pltpu.VMEM((PACK_ROWS * TILE_STRIDE, LANES), jnp.int32), pltpu.VMEM((tm, 2 * nsel), F32),
                        pltpu.VMEM((tm, 2 * nsel), F32), pltpu.VMEM((tm, d), F32)],
        compiler_params=pltpu.CompilerParams(dimension_semantics=("parallel",), vmem_limit_bytes=VMEM_LIMIT),
        name="peer_v",
    )(idx, w, x2, gfin, tab)


def _t5_bucket(rel):
    nb = REL_BUCKETS // 2
    max_exact = nb // 2
    ret = jnp.where(rel > 0, nb, 0)
    n = jnp.abs(rel)
    large = max_exact + (jnp.log(jnp.maximum(n, 1).astype(F32) / max_exact)
                         / math.log(REL_MAX_DIST / max_exact) * (nb - max_exact)).astype(jnp.int32)
    large = jnp.minimum(large, nb - 1)
    return ret + jnp.where(n < max_exact, n, large)


def _bias_tiles(rel_bias, q_pos, k_pos, k_valid):
    bucket = _t5_bucket(k_pos[None, :] - q_pos[:, None])
    table = rel_bias.astype(F32) * LOG2E
    bias = jnp.zeros((rel_bias.shape[1],) + bucket.shape, F32)
    for bkt in range(REL_BUCKETS):
        bias = jnp.where((bucket == bkt)[None], table[bkt][:, None, None], bias)
    visible = ((k_pos // CHUNK)[None, :] <= (q_pos // CHUNK)[:, None]) & k_valid[None, :]
    return jnp.where(visible[None], bias, NEG_INF)


def _rope_tables(pos):
    half = MLA_ROPE // 2
    inv = ROPE_THETA ** (-jnp.arange(half, dtype=F32) / half)
    ang = pos.astype(F32)[:, None] * inv
    cos, sin = jnp.cos(ang), jnp.sin(ang)
    pad = jnp.zeros((pos.shape[0], LANES - MLA_ROPE), F32)
    return jnp.concatenate([cos, cos, pad], axis=1), jnp.concatenate([-sin, sin, pad], axis=1)


def _pack_table(tab):
    n, d = tab.shape
    bits = lax.bitcast_convert_type(tab.astype(BF16), jnp.uint16).astype(jnp.uint32)
    words = bits[:, :d // 2] | (bits[:, d // 2:] << 16)
    return lax.bitcast_convert_type(words, jnp.int32).reshape(n * PACK_ROWS, LANES)


def _layer_weights(l, norm_mix, w_in, diff_lambda, diff_subln, mla_q_norm, mla_w_uq, mla_kv_norm, mla_w_uk, mla_w_uv,
                   w_branch_a, w_branch_b, w_out, norm_ffn, peer_w_q, peer_keys, peer_u, peer_v):
    d = w_in.shape[1]
    da = DA_HEADS * 2 * DA_HEAD_DIM
    q_lora = mla_w_uq.shape[1]
    kv_lora = mla_w_uk.shape[1]
    nh = mla_w_uq.shape[2]
    win = w_in[l]
    c0 = 3 * da
    cuts = [c0, c0 + q_lora, c0 + q_lora + kv_lora, c0 + q_lora + kv_lora + MLA_ROPE]
    qscale = jnp.concatenate([jnp.full((da,), DA_HEAD_DIM ** -0.5 * LOG2E, F32), jnp.ones((2 * da,), F32)])
    w = {}
    w["gmix"] = norm_mix[l][None]
    w["wqkv"] = (win[:, :c0] * qscale).astype(BF16)
    w["wcq"] = win[:, cuts[0]:cuts[1]].astype(BF16)
    w["wkv"] = jnp.concatenate([win[:, cuts[1]:cuts[3]], jnp.zeros((d, LANES - MLA_ROPE), F32)], axis=1).astype(BF16)
    w["wg"] = win[:, cuts[3]:].astype(BF16)
    w["gq"] = mla_q_norm[l][None]
    w["gkv"] = mla_kv_norm[l][None]
    uq = jnp.transpose(mla_w_uq[l], (1, 0, 2))
    uk = jnp.transpose(mla_w_uk[l], (1, 2, 0))
    wlat = _bmm(uq[:, :, :MLA_NOPE], uk)
    mla_scale = (MLA_NOPE + MLA_ROPE) ** -0.5 * LOG2E
    wqf = jnp.concatenate([wlat, uq[:, :, MLA_NOPE:], jnp.zeros((nh, q_lora, LANES - MLA_ROPE), F32)], axis=2) * mla_scale
    w["wqf"] = jnp.transpose(wqf, (1, 0, 2)).reshape(q_lora, nh * 2 * LANES).astype(BF16)
    w["lam"] = (jnp.exp(jnp.sum(diff_lambda[l][0] * diff_lambda[l][1]))
                - jnp.exp(jnp.sum(diff_lambda[l][2] * diff_lambda[l][3]))).astype(F32)
    w["subln"] = diff_subln[l][None]
    w["wa"] = w_branch_a[l].astype(BF16)
    uv = jnp.transpose(mla_w_uv[l], (1, 0, 2))
    wbb = w_branch_b[l].reshape(nh, uv.shape[2], d)
    w["wb"] = _bmm(uv, wbb).reshape(nh * kv_lora, d).astype(BF16)
    w["wo"] = w_out[l].astype(BF16)
    w["gffn"] = norm_ffn[l][None]
    ng = PEER_HEADS * 2
    dk = peer_keys.shape[-1]
    keys = peer_keys[l].reshape(ng, PEER_N_KEYS, dk)
    wq_t = jnp.transpose(peer_w_q[l]).reshape(ng, dk, d)
    w["wst"] = _bmm(keys, wq_t).reshape(ng * PEER_N_KEYS, d).astype(BF16)
    w["utab"] = _pack_table(peer_u[l])
    w["vtab"] = _pack_table(peer_v[l])
    return w


def _group_forward(x, pos, past, layer_idx, rel_bias, w, gfin, *, tm, tm_post, tq_diff, tq_mla, rs_diff, rs_mla,
                   tm_peer):
    b, s, d = x.shape
    t = b * s
    lambda_init = 0.8 - 0.6 * math.exp(-0.3 * layer_idx)
    cos_t, sin_t = _rope_tables(pos)
    xf = x.reshape(t, d)
    dq, dk, dv, dkb, dvb, qm, ckv, kpe, kf, gate = _in_proj(
        xf, cos_t, sin_t, w["gmix"], w["wqkv"], w["wcq"], w["wkv"], w["wg"], w["gq"], w["gkv"], w["wqf"], seq=s, tm=tm)
    lam = jnp.reshape(w["lam"] + lambda_init, (1,))
    nb = REL_BUCKETS // 2
    cfar = rel_bias[nb - 1].astype(F32) * LOG2E
    if past is None:
        assert s % tq_diff == 0 and s % tq_mla == 0 and tq_diff % CHUNK == 0 and tq_mla % CHUNK == 0
        assert tq_diff >= REL_MAX_DIST
        tq = tq_diff
        q_pos = jnp.arange(tq, 2 * tq, dtype=jnp.int32)
        k_pos = jnp.arange(0, 2 * tq, dtype=jnp.int32)
        both = _bias_tiles(rel_bias, q_pos, k_pos, jnp.ones((2 * tq,), bool))
        bias = jnp.stack([both[:, :, :tq], both[:, :, tq:]], axis=1)
        oa = _diff_attn(dq.reshape(b, s, d), dkb.reshape(b, s, d), dvb.reshape(b, s, d), bias, cfar, lam, w["subln"],
                        tq=tq, tk=tq, rs=rs_diff, causal=True, out_scale=1.0 - lambda_init)
        tqm = tq_mla
        lp = jnp.arange(tqm, dtype=jnp.int32)
        mask = jnp.where((lp // CHUNK)[None, :] <= (lp // CHUNK)[:, None], 0.0, NEG_INF).astype(F32)
        ol = _mla_attn(qm.reshape(b, s, -1), kf.reshape(b, s, -1), mask, tq=tqm, tk=tqm, rs=rs_mla,
                       nh=MLA_HEADS // 2, causal=True)
    else:
        pk, pv, pckv, pkpe = past
        p = pk.shape[1]
        sk = -(-(p + s) // LANES) * LANES
        npad = sk - p - s
        k_pos = jnp.concatenate([jnp.arange(p, dtype=jnp.int32), pos, jnp.zeros((npad,), jnp.int32)])
        k_valid = jnp.arange(sk) < p + s
        zpad = jnp.zeros((b, npad, d), BF16)
        k_all = jnp.concatenate([pk.reshape(b, p, d).astype(BF16), dkb.reshape(b, s, d), zpad], axis=1)
        v_all = jnp.concatenate([pv.reshape(b, p, d).astype(BF16), dvb.reshape(b, s, d), zpad], axis=1)
        bias = _bias_tiles(rel_bias, pos, k_pos, k_valid)[:, None]
        oa = _diff_attn(dq.reshape(b, s, d), k_all, v_all, bias, cfar, lam, w["subln"],
                        tq=s, tk=sk, rs=rs_diff, causal=False, out_scale=1.0 - lambda_init)
        pad = jnp.zeros((b, p, LANES - MLA_ROPE), BF16)
        kf_past = jnp.concatenate([pckv.astype(BF16), pkpe.astype(BF16), pad], axis=2)
        kf_all = jnp.concatenate([kf_past, kf.reshape(b, s, -1), jnp.zeros((b, npad, 2 * LANES), BF16)], axis=1)
        mask = jnp.broadcast_to(jnp.where(k_valid, 0.0, NEG_INF).astype(F32)[None], (s, sk))
        ol = _mla_attn(qm.reshape(b, s, -1), kf_all, mask, tq=s, tk=sk, rs=rs_mla, nh=MLA_HEADS, causal=False)
    x2, h2, st = _post(xf, oa.reshape(t, d), ol.reshape(t, -1), gate, w["wa"], w["wb"], w["wo"], w["gffn"], w["wst"],
                       tm=tm_post)
    idx_t, g_t = _topk(st, tm=tm_peer)
    idx = jnp.transpose(idx_t)
    wts = _peer_u(idx, h2, jnp.transpose(g_t), w["utab"], tm=tm_peer)
    y = _peer_v(idx, wts, x2, gfin, w["vtab"], tm=tm_peer)
    return y.reshape(b, s, d), (dk, dv, ckv, kpe)


def kernel(x_prompt, x_sample, cache_diff_k, cache_diff_v, cache_mla_ckv, cache_mla_kpe, rel_bias, norm_mix, w_in,
           diff_lambda, diff_subln, mla_q_norm, mla_w_uq, mla_kv_norm, mla_w_uk, mla_w_uv, w_branch_a, w_branch_b,
           w_out, norm_ffn, peer_w_q, peer_keys, peer_u, peer_v, norm_final):
    depth = w_in.shape[0]
    assert depth == 1, "the final RMSNorm is fused into the last layer's PEER kernel"
    bp, sp, d = x_prompt.shape
    bs, ss, _ = x_sample.shape
    past_len = cache_diff_k.shape[2]
    pos_p = jnp.arange(sp, dtype=jnp.int32)
    pos_s = past_len + jnp.arange(ss, dtype=jnp.int32)
    gfin = norm_final[None]
    l = 0
    w = _layer_weights(l, norm_mix, w_in, diff_lambda, diff_subln, mla_q_norm, mla_w_uq, mla_kv_norm, mla_w_uk,
                       mla_w_uv, w_branch_a, w_branch_b, w_out, norm_ffn, peer_w_q, peer_keys, peer_u, peer_v)
    yp, (kp, vp, cp, ep) = _group_forward(x_prompt, pos_p, None, l, rel_bias, w, gfin,
                                          tm=min(256, sp), tm_post=256, tq_diff=min(512, sp), tq_mla=min(512, sp),
                                          rs_diff=128, rs_mla=128, tm_peer=128)
    past = (cache_diff_k[l], cache_diff_v[l], cache_mla_ckv[l], cache_mla_kpe[l])
    ys, (ks, vs, cs, es) = _group_forward(x_sample, pos_s, past, l, rel_bias, w, gfin,
                                          tm=ss, tm_post=128, tq_diff=ss, tq_mla=ss, rs_diff=ss, rs_mla=ss,
                                          tm_peer=128)
    nh, hd = DA_HEADS, DA_HEAD_DIM
    return (yp, ys,
            kp.reshape(1, bp, sp, nh, 2, hd), vp.reshape(1, bp, sp, nh, 2 * hd),
            cp.reshape(1, bp, sp, -1), ep.reshape(1, bp, sp, -1),
            ks.reshape(1, bs, ss, nh, 2, hd), vs.reshape(1, bs, ss, nh, 2 * hd),
            cs.reshape(1, bs, ss, -1), es.reshape(1, bs, ss, -1))
```

```python
import functools
import math

import jax
import jax.numpy as jnp
from jax import lax
from jax.experimental import pallas as pl
from jax.experimental.pallas import tpu as pltpu

F32 = jnp.float32
BF16 = jnp.bfloat16

CHUNK = 64
DA_HEADS = 8
DA_HEAD_DIM = 64
MLA_HEADS = 16
MLA_NOPE = 64
MLA_ROPE = 32
ROPE_THETA = 10000.0
REL_BUCKETS = 32
REL_MAX_DIST = 128
PEER_HEADS = 8
PEER_N_KEYS = 128
PEER_TOPK = 16
NORM_EPS = 1e-6
NEG_INF = -1e30
LOG2E = math.log2(math.e)

LANES = 128
PACK_ROWS = 4
TILE_STRIDE = 136
VMEM_LIMIT = 56 * 1024 * 1024
GROUPS_PER_ITER = 2
GATHER_TILES = 4
GATHER_LEAD = 2
IDX_GROUP = 8
IDX_SLOTS = 4


def _const_spec(shape):
    nd = len(shape)
    return pl.BlockSpec(shape, lambda *_: (0,) * nd, pipeline_mode=pl.Buffered(1))


def _rms(x, g):
    return x * lax.rsqrt(jnp.mean(x * x, axis=-1, keepdims=True) + NORM_EPS) * g


def _dot(a, b):
    return jnp.dot(a, b, preferred_element_type=F32)


def _dot_nt(a, b):
    return lax.dot_general(a, b, (((1,), (1,)), ((), ())), preferred_element_type=F32)


def _bmm_kernel(a_ref, b_ref, o_ref):
    o_ref[...] = jnp.dot(a_ref[...], b_ref[...], precision=lax.Precision.HIGHEST, preferred_element_type=F32)


def _bmm(a, b):
    n, m, k = a.shape
    _, _, p = b.shape
    return pl.pallas_call(
        _bmm_kernel,
        grid=(n,),
        in_specs=[pl.BlockSpec((None, m, k), lambda i: (i, 0, 0)), pl.BlockSpec((None, k, p), lambda i: (i, 0, 0))],
        out_specs=pl.BlockSpec((None, m, p), lambda i: (i, 0, 0)),
        out_shape=jax.ShapeDtypeStruct((n, m, p), F32),
        name="fold_bmm",
    )(a, b)


def _in_proj_kernel(x_ref, cos_ref, sin_ref, gmix_ref, wqkv_ref, wcq_ref, wkv_ref, wg_ref, gq_ref, gkv_ref, wqf_ref,
                    dq_ref, dk_ref, dv_ref, dkb_ref, dvb_ref, qm_ref, ckv_ref, kpe_ref, kf_ref, gate_ref, *, d, nh):
    hb = _rms(x_ref[...], gmix_ref[...]).astype(BF16)
    z = _dot(hb, wqkv_ref[...])
    dq_ref[...] = z[:, :d].astype(BF16)
    dk = z[:, d:2 * d]
    dk_ref[...] = dk
    dkb_ref[...] = dk.astype(BF16)
    dv = z[:, 2 * d:]
    dv_ref[...] = dv
    dvb_ref[...] = dv.astype(BF16)

    cosb = cos_ref[...]
    sinb = sin_ref[...]
    lane = lax.broadcasted_iota(jnp.int32, cosb.shape, 1)
    half = MLA_ROPE // 2

    def rope(blk):
        swapped = jnp.where(lane < half, pltpu.roll(blk, LANES - half, 1), pltpu.roll(blk, half, 1))
        return blk * cosb + swapped * sinb

    zkv = _dot(hb, wkv_ref[...])
    ckv = _rms(zkv[:, :LANES], gkv_ref[...])
    ckv_ref[...] = ckv
    kr = rope(zkv[:, LANES:])
    kpe_ref[...] = kr[:, :MLA_ROPE]
    kf_ref[...] = jnp.concatenate([ckv.astype(BF16), kr.astype(BF16)], axis=1)

    cq = _rms(_dot(hb, wcq_ref[...]), gq_ref[...]).astype(BF16)
    qraw = _dot(cq, wqf_ref[...])
    for h in range(nh):
        c0 = 2 * LANES * h
        qm_ref[:, c0:c0 + LANES] = qraw[:, c0:c0 + LANES].astype(BF16)
        qm_ref[:, c0 + LANES:c0 + 2 * LANES] = rope(qraw[:, c0 + LANES:c0 + 2 * LANES]).astype(BF16)

    zg = _dot(hb, wg_ref[...])
    gate_ref[...] = (1.0 / (1.0 + jnp.exp(-zg))).astype(BF16)


def _in_proj(x, cos_t, sin_t, gmix, wqkv, wcq, wkv, wg, gq, gkv, wqf, *, seq, tm):
    t, d = x.shape
    nh = wqf.shape[1] // (2 * LANES)
    nblk = seq // tm
    tok = lambda w: pl.BlockSpec((tm, w), lambda i: (i, 0))
    pos = pl.BlockSpec((tm, LANES), lambda i: (i % nblk, 0))
    outs = [
        (d, BF16), (d, F32), (d, F32), (d, BF16), (d, BF16), (wqf.shape[1], BF16), (LANES, F32), (MLA_ROPE, F32),
        (2 * LANES, BF16), (2 * d, BF16),
    ]
    return pl.pallas_call(
        functools.partial(_in_proj_kernel, d=d, nh=nh),
        grid=(t // tm,),
        in_specs=[tok(d), pos, pos, _const_spec(gmix.shape), _const_spec(wqkv.shape), _const_spec(wcq.shape),
                  _const_spec(wkv.shape), _const_spec(wg.shape), _const_spec(gq.shape), _const_spec(gkv.shape),
                  _const_spec(wqf.shape)],
        out_specs=[tok(w) for w, _ in outs],
        out_shape=[jax.ShapeDtypeStruct((t, w), dt) for w, dt in outs],
        compiler_params=pltpu.CompilerParams(dimension_semantics=("parallel",), vmem_limit_bytes=VMEM_LIMIT),
        name="in_proj",
    )(x, cos_t, sin_t, gmix, wqkv, wcq, wkv, wg, gq, gkv, wqf)


def _attn_step(q_all, kb, vb, bias_fn, state, *, rs):
    s_ref, p_ref, m_ref, l_ref, a_ref, acc_ref = state
    rows, tk = s_ref.shape
    nrep = tk // LANES
    s_ref[...] = _dot_nt(q_all, kb)

    def groups(r, c):
        for u in range(GROUPS_PER_ITER):
            r0 = pl.multiple_of((r * GROUPS_PER_ITER + u) * rs, rs)
            s = s_ref[pl.ds(r0, rs), :]
            if bias_fn is not None:
                s = bias_fn(s, r0)
            m_old = m_ref[pl.ds(r0, rs), :]
            m_new = jnp.maximum(m_old, jnp.max(s, axis=-1, keepdims=True))
            alpha = jnp.exp2(m_old - m_new)
            p = jnp.exp2(s - jnp.concatenate([m_new] * nrep, axis=1))
            psum = p[:, :LANES]
            for j in range(1, nrep):
                psum = psum + p[:, j * LANES:(j + 1) * LANES]
            l_ref[pl.ds(r0, rs), :] = alpha * l_ref[pl.ds(r0, rs), :] + psum
            m_ref[pl.ds(r0, rs), :] = m_new
            a_ref[pl.ds(r0, rs), :] = alpha
            p_ref[pl.ds(r0, rs), :] = p.astype(BF16)
        return c

    assert rows % (rs * GROUPS_PER_ITER) == 0
    lax.fori_loop(0, rows // (rs * GROUPS_PER_ITER), groups, 0)
    acc_ref[...] = a_ref[...] * acc_ref[...] + _dot(p_ref[...], vb)


def _init_state(state):
    _, _, m_ref, l_ref, _, acc_ref = state
    m_ref[...] = jnp.full(m_ref.shape, -jnp.inf, F32)
    l_ref[...] = jnp.zeros(l_ref.shape, F32)
    acc_ref[...] = jnp.zeros(acc_ref.shape, F32)


def _attn_state_shapes(rows, tk):
    return [pltpu.VMEM((rows, tk), F32), pltpu.VMEM((rows, tk), BF16), pltpu.VMEM((rows, LANES), F32),
            pltpu.VMEM((rows, LANES), F32), pltpu.VMEM((rows, LANES), F32), pltpu.VMEM((rows, LANES), F32)]


def _attn_finish(state):
    _, _, _, l_ref, _, acc_ref = state
    return acc_ref[...] / jnp.sum(l_ref[...], axis=-1, keepdims=True)


def _diff_attn_kernel(cfar_ref, lam_ref, q_ref, k_ref, v_ref, bias_ref, g_ref, o_ref, q2_ref, *state,
                      tq, tk, rs, causal, out_scale):
    h = pl.program_id(1)
    i = pl.program_id(2)
    q = q_ref[...]
    lane = lax.broadcasted_iota(jnp.int32, q.shape, 1)
    zero = jnp.zeros_like(q)
    q2_ref[:tq] = jnp.where(lane < DA_HEAD_DIM, q, zero)
    q2_ref[tq:] = jnp.where(lane >= DA_HEAD_DIM, q, zero)
    _init_state(state)

    def tile_bias(n):
        def fn(s, r0):
            b0 = pl.multiple_of(r0 - jnp.where(r0 >= tq, tq, 0), rs)
            return s + bias_ref[n, pl.ds(b0, rs), :]
        return fn

    def step(off, bias_fn):
        _attn_step(q2_ref[...], k_ref[pl.ds(off, tk), :], v_ref[pl.ds(off, tk), :], bias_fn, state, rs=rs)

    if causal:
        cfar = cfar_ref[h]

        def far(j, c):
            step(pl.multiple_of(j * tk, tk), lambda s, r0: s + cfar)
            return c

        lax.fori_loop(0, jnp.maximum(i - 1, 0), far, 0)

        @pl.when(i > 0)
        def _():
            step(pl.multiple_of((i - 1) * tk, tk), tile_bias(0))

        step(pl.multiple_of(i * tk, tk), tile_bias(1))
    else:
        step(0, tile_bias(0))

    o = _attn_finish(state)
    od = o[:tq] - lam_ref[0] * o[tq:]
    o_ref[...] = (_rms(od, g_ref[...]) * out_scale).astype(BF16)


def _diff_attn(q, k, v, bias, cfar, lam, g, *, tq, tk, rs, causal, out_scale):
    b, sq, d = q.shape
    sk = k.shape[1]
    nh = d // LANES
    n_near = bias.shape[1]
    assert tq % rs == 0 and tk % LANES == 0 and sk % tk == 0
    smem = pl.BlockSpec(memory_space=pltpu.SMEM)
    return pl.pallas_call(
        functools.partial(_diff_attn_kernel, tq=tq, tk=tk, rs=rs, causal=causal, out_scale=out_scale),
        grid=(b, nh, sq // tq),
        in_specs=[smem, smem,
                  pl.BlockSpec((None, tq, LANES), lambda bb, hh, ii: (bb, ii, hh)),
                  pl.BlockSpec((None, sk, LANES), lambda bb, hh, ii: (bb, 0, hh)),
                  pl.BlockSpec((None, sk, LANES), lambda bb, hh, ii: (bb, 0, hh)),
                  pl.BlockSpec((None, n_near, tq, bias.shape[3]), lambda bb, hh, ii: (hh, 0, 0, 0)),
                  pl.BlockSpec((1, LANES), lambda bb, hh, ii: (0, 0))],
        out_specs=pl.BlockSpec((None, tq, LANES), lambda bb, hh, ii: (bb, ii, hh)),
        out_shape=jax.ShapeDtypeStruct((b, sq, d), BF16),
        scratch_shapes=[pltpu.VMEM((2 * tq, LANES), BF16)] + _attn_state_shapes(2 * tq, tk),
        compiler_params=pltpu.CompilerParams(dimension_semantics=("parallel", "parallel", "parallel"),
                                             vmem_limit_bytes=VMEM_LIMIT),
        name="diff_attn",
    )(cfar, lam, q, k, v, bias, g)


def _mla_attn_kernel(q_ref, kf_ref, mask_ref, o_ref, qs_ref, *state, tq, tk, rs, nh, causal):
    i = pl.program_id(2)
    for h in range(nh):
        qs_ref[h * tq:(h + 1) * tq, :] = q_ref[:, 2 * LANES * h:2 * LANES * (h + 1)]
    _init_state(state)

    def mask_fn(s, r0):
        b0 = pl.multiple_of(lax.rem(r0, tq), rs)
        return s + mask_ref[pl.ds(b0, rs), :]

    def step(off, bias_fn):
        _attn_step(qs_ref[...], kf_ref[pl.ds(off, tk), :], kf_ref[pl.ds(off, tk), :LANES], bias_fn, state, rs=rs)

    if causal:
        def far(j, c):
            step(pl.multiple_of(j * tk, tk), None)
            return c

        lax.fori_loop(0, i, far, 0)
        step(pl.multiple_of(i * tk, tk), mask_fn)
    else:
        step(0, mask_fn)

    o = _attn_finish(state)
    for h in range(nh):
        o_ref[:, LANES * h:LANES * (h + 1)] = o[h * tq:(h + 1) * tq].astype(BF16)


def _mla_attn(q, kf, mask, *, tq, tk, rs, nh, causal):
    b, sq, w = q.shape
    sk = kf.shape[1]
    ngrp = w // (2 * LANES * nh)
    assert tq % rs == 0 and tk % LANES == 0 and sk % tk == 0 and mask.shape == (tq, tk)
    return pl.pallas_call(
        functools.partial(_mla_attn_kernel, tq=tq, tk=tk, rs=rs, nh=nh, causal=causal),
        grid=(b, ngrp, sq // tq),
        in_specs=[pl.BlockSpec((None, tq, nh * 2 * LANES), lambda bb, gg, ii: (bb, ii, gg)),
                  pl.BlockSpec((None, sk, 2 * LANES), lambda bb, gg, ii: (bb, 0, 0)),
                  pl.BlockSpec(mask.shape, lambda bb, gg, ii: (0, 0))],
        out_specs=pl.BlockSpec((None, tq, nh * LANES), lambda bb, gg, ii: (bb, ii, gg)),
        out_shape=jax.ShapeDtypeStruct((b, sq, ngrp * nh * LANES), BF16),
        scratch_shapes=[pltpu.VMEM((nh * tq, 2 * LANES), BF16)] + _attn_state_shapes(nh * tq, tk),
        compiler_params=pltpu.CompilerParams(dimension_semantics=("parallel", "parallel", "parallel"),
                                             vmem_limit_bytes=VMEM_LIMIT),
        name="mla_attn",
    )(q, kf, mask)


def _post_kernel(x_ref, oa_ref, ol_ref, gate_ref, wa_ref, wb_ref, wo_ref, gffn_ref, wst_ref, x2_ref, h2_ref, st_ref,
                 *, d):
    a = _dot(oa_ref[...], wa_ref[...])
    b = _dot(ol_ref[...], wb_ref[...])
    g = gate_ref[...].astype(F32)
    merged = g[:, :d] * a + g[:, d:] * b
    x2 = x_ref[...] + _dot(merged.astype(BF16), wo_ref[...])
    x2_ref[...] = x2
    h2 = _rms(x2, gffn_ref[...])
    h2_ref[...] = h2
    st_ref[...] = _dot_nt(wst_ref[...], h2.astype(BF16))


def _post(x, oa, ol, gate, wa, wb, wo, gffn, wst, *, tm):
    t, d = x.shape
    ns = wst.shape[0]
    tok = lambda w: pl.BlockSpec((tm, w), lambda i: (i, 0))
    return pl.pallas_call(
        functools.partial(_post_kernel, d=d),
        grid=(t // tm,),
        in_specs=[tok(d), tok(oa.shape[1]), tok(ol.shape[1]), tok(gate.shape[1]), _const_spec(wa.shape),
                  _const_spec(wb.shape), _const_spec(wo.shape), _const_spec(gffn.shape), _const_spec(wst.shape)],
        out_specs=[tok(d), tok(d), pl.BlockSpec((ns, tm), lambda i: (0, i))],
        out_shape=[jax.ShapeDtypeStruct((t, d), F32), jax.ShapeDtypeStruct((t, d), F32),
                   jax.ShapeDtypeStruct((ns, t), F32)],
        compiler_params=pltpu.CompilerParams(dimension_semantics=("parallel",), vmem_limit_bytes=VMEM_LIMIT),
        name="post",
    )(x, oa, ol, gate, wa, wb, wo, gffn, wst)


def _top_rows(s, order=None, payload=None):
    if order is None:
        order = lax.broadcasted_iota(jnp.int32, s.shape, 0)
    big = jnp.int32(2 ** 30)
    vals, picks = [], []
    for _ in range(PEER_TOPK):
        m = jnp.max(s, axis=0, keepdims=True)
        pos = jnp.min(jnp.where(s == m, order, big), axis=0, keepdims=True)
        sel = order == pos
        vals.append(m)
        picks.append(pos if payload is None else jnp.sum(jnp.where(sel, payload, 0), axis=0, keepdims=True))
        s = jnp.where(sel, -jnp.inf, s)
    return jnp.concatenate(vals, axis=0), jnp.concatenate(picks, axis=0)


def _pair_candidates(va, ia, vb, ib):
    k, nk = PEER_TOPK, PEER_N_KEYS
    split = 4
    unused = jnp.int32(2 ** 29)
    vals, orders, idxs = [], [], []
    for a in range(split):
        n = k // (a + 1)
        rows = -(-n // 8) * 8
        r = lax.broadcasted_iota(jnp.int32, (rows, va.shape[1]), 0)
        valid = r < n
        vals.append(jnp.where(valid, va[a:a + 1] + vb[:rows], -jnp.inf))
        orders.append(jnp.where(valid, a * k + r, unused))
        idxs.append(ia[a:a + 1] * nk + ib[:rows])
    for b in range(k // (split + 1)):
        n = k // (b + 1)
        rows = -(-n // 8) * 8
        r = lax.broadcasted_iota(jnp.int32, (rows, va.shape[1]), 0)
        valid = (r >= split) & (r < n)
        vals.append(jnp.where(valid, va[:rows] + vb[b:b + 1], -jnp.inf))
        orders.append(jnp.where(valid, r * k + b, unused))
        idxs.append(ia[:rows] * nk + ib[b:b + 1])
    return jnp.concatenate(vals, axis=0), jnp.concatenate(orders, axis=0), jnp.concatenate(idxs, axis=0)


def _topk_kernel(st_ref, idx_ref, g_ref):
    nk = PEER_N_KEYS

    def head(h, c):
        base = pl.multiple_of(h * 2 * nk, 2 * nk)
        va, ia = _top_rows(st_ref[pl.ds(base, nk), :])
        vb, ib = _top_rows(st_ref[pl.ds(base + nk, nk), :])
        cand, order, cidx = _pair_candidates(va, ia, vb, ib)
        tv, ti = _top_rows(cand, order, cidx)
        e = jnp.exp(tv - tv[0:1])
        ob = pl.multiple_of(h * PEER_TOPK, PEER_TOPK)
        idx_ref[pl.ds(ob, PEER_TOPK), :] = ti * PACK_ROWS
        g_ref[pl.ds(ob, PEER_TOPK), :] = e / jnp.sum(e, axis=0, keepdims=True)
        return c

    lax.fori_loop(0, PEER_HEADS, head, 0)


def _topk(st, *, tm):
    ns, t = st.shape
    nsel = PEER_HEADS * PEER_TOPK
    return pl.pallas_call(
        _topk_kernel,
        grid=(t // tm,),
        in_specs=[pl.BlockSpec((ns, tm), lambda i: (0, i))],
        out_specs=[pl.BlockSpec((nsel, tm), lambda i: (0, i)), pl.BlockSpec((nsel, tm), lambda i: (0, i))],
        out_shape=[jax.ShapeDtypeStruct((nsel, t), jnp.int32), jax.ShapeDtypeStruct((nsel, t), F32)],
        compiler_params=pltpu.CompilerParams(dimension_semantics=("parallel",), vmem_limit_bytes=VMEM_LIMIT),
        name="topk",
    )(st)


def _gather_scratch(nsel):
    return ([pltpu.SMEM((IDX_SLOTS, IDX_GROUP, nsel), jnp.int32), pltpu.SemaphoreType.DMA((IDX_SLOTS,))]
            + [pltpu.VMEM((PACK_ROWS * TILE_STRIDE, LANES), jnp.int32) for _ in range(GATHER_TILES)])


def _tile_rows(tile_ref, nsel):
    x = jnp.concatenate([tile_ref[q * TILE_STRIDE:q * TILE_STRIDE + nsel, :] for q in range(PACK_ROWS)], axis=1)
    return pltpu.bitcast(x, BF16)


def _pipelined_tokens(tm, nsel, idx_hbm, idx_s, sems, tab_ref, tiles, compute):
    n = len(tiles)
    span = IDX_SLOTS * IDX_GROUP
    groups = tm // IDX_GROUP
    assert tm % span == 0 and span % n == 0 and GATHER_LEAD < min(n, IDX_GROUP)
    g0 = pl.program_id(0) * groups

    def idx_copy(group, slot):
        g = g0 + jnp.minimum(group, groups - 1)
        return pltpu.make_async_copy(idx_hbm.at[g], idx_s.at[slot], sems.at[slot])

    def gather(pos, tile_ref):
        slot, row = (pos // IDX_GROUP) % IDX_SLOTS, pos % IDX_GROUP
        for mi in range(nsel):
            i = pl.multiple_of(idx_s[slot, row, mi], PACK_ROWS)
            tile_ref[pl.ds(mi, PACK_ROWS, stride=TILE_STRIDE), :] = tab_ref[pl.ds(i, PACK_ROWS), :]

    for s in range(IDX_SLOTS):
        idx_copy(s, s).start()
    idx_copy(0, 0).wait()
    for j in range(GATHER_LEAD):
        gather(j, tiles[j % n])

    def body(b, c):
        for j in range(span):
            pos = j + GATHER_LEAD
            slot = (pos // IDX_GROUP) % IDX_SLOTS
            group = b * IDX_SLOTS + pos // IDX_GROUP
            if pos % IDX_GROUP == 0:
                idx_copy(group, slot).wait()
            gather(pos, tiles[pos % n])
            if pos % IDX_GROUP == IDX_GROUP - 1:
                idx_copy(group + IDX_SLOTS, slot).start()
            compute(b * span + j, tiles[j % n])
        return c

    lax.fori_loop(0, tm // span, body, 0)
    for s in range(1, IDX_SLOTS):
        idx_copy(0, s).wait()


def _two_rows(lo, hi):
    row = lax.broadcasted_iota(jnp.int32, (8, lo.shape[1]), 0)
    return jnp.where(row < 4, jnp.broadcast_to(lo, (8, lo.shape[1])), jnp.broadcast_to(hi, (8, hi.shape[1]))).astype(BF16)


def _peer_u_kernel(idx_hbm, h_ref, g_ref, tab_ref, w_ref, z_ref, idx_s, sems, *tiles, tm, nsel):
    half = h_ref.shape[1] // 2

    def compute(t, tile_ref):
        hrow = h_ref[pl.ds(t, 1), :]
        z8 = _dot_nt(_two_rows(hrow[:, :half], hrow[:, half:]), _tile_rows(tile_ref, nsel))
        lane = lax.broadcasted_iota(jnp.int32, (1, 2 * nsel), 1)
        z_ref[pl.ds(t, 1), :] = jnp.where(lane % 2 == 0, z8[0:1], z8[4:5])

    _pipelined_tokens(tm, nsel, idx_hbm, idx_s, sems, tab_ref, tiles, compute)
    r = lax.broadcasted_iota(jnp.int32, (2 * nsel, nsel), 0)
    col = lax.broadcasted_iota(jnp.int32, (2 * nsel, nsel), 1)
    pair = (r // 2 == col).astype(F32)
    act = jnp.dot(z_ref[...], pair, precision=lax.Precision.HIGHEST, preferred_element_type=F32)
    gelu = 0.5 * act * (1.0 + lax.erf(act * (1.0 / math.sqrt(2.0))))
    w_ref[...] = g_ref[...] * gelu


def _peer_u(idx, h2, g, tab, *, tm):
    t, d = h2.shape
    nsel = idx.shape[2]
    tok = lambda w: pl.BlockSpec((tm, w), lambda i: (i, 0))
    return pl.pallas_call(
        functools.partial(_peer_u_kernel, tm=tm, nsel=nsel),
        grid=(t // tm,),
        in_specs=[pl.BlockSpec(memory_space=pl.ANY), tok(d), tok(nsel), _const_spec(tab.shape)],
        out_specs=tok(nsel),
        out_shape=jax.ShapeDtypeStruct((t, nsel), F32),
        scratch_shapes=[pltpu.VMEM((tm, 2 * nsel), F32)] + _gather_scratch(nsel),
        compiler_params=pltpu.CompilerParams(dimension_semantics=("parallel",), vmem_limit_bytes=VMEM_LIMIT),
        name="peer_u",
    )(idx, h2, g, tab)


def _peer_v_kernel(idx_hbm, w_ref, x2_ref, gfin_ref, tab_ref, y_ref, we_ref, wo_ref, po_ref, idx_s, sems, *tiles,
                   tm, nsel):
    wb = w_ref[...].astype(BF16)
    r = lax.broadcasted_iota(jnp.int32, (nsel, 2 * nsel), 0)
    col = lax.broadcasted_iota(jnp.int32, (nsel, 2 * nsel), 1)
    we_ref[...] = _dot(wb, (col == 2 * r).astype(BF16))
    wo_ref[...] = _dot(wb, (col == 2 * r + 1).astype(BF16))

    def compute(t, tile_ref):
        o8 = _dot(_two_rows(we_ref[pl.ds(t, 1), :], wo_ref[pl.ds(t, 1), :]), _tile_rows(tile_ref, nsel))
        po_ref[pl.ds(t, 1), :] = jnp.concatenate([o8[0:1], o8[4:5]], axis=1)

    _pipelined_tokens(tm, nsel, idx_hbm, idx_s, sems, tab_ref, tiles, compute)
    y_ref[...] = _rms(x2_ref[...] + po_ref[...], gfin_ref[...])


def _peer_v(idx, w, x2, gfin, tab, *, tm):
    t, d = x2.shape
    nsel = idx.shape[2]
    tok = lambda wd: pl.BlockSpec((tm, wd), lambda i: (i, 0))
    return pl.pallas_call(
        functools.partial(_peer_v_kernel, tm=tm, nsel=nsel),
        grid=(t // tm,),
        in_specs=[pl.BlockSpec(memory_space=pl.ANY), tok(nsel), tok(d), _const_spec(gfin.shape),
                  _const_spec(tab.shape)],
        out_specs=tok(d),
        out_shape=jax.ShapeDtypeStruct((t, d), F32),
        scratch_shapes=[pltpu.VMEM((tm, 2 * nsel), F32), pltpu.VMEM((tm, 2 * nsel), F32), pltpu.VMEM((tm, d), F32)]
        + _gather_scratch(nsel),
        compiler_params=pltpu.CompilerParams(dimension_semantics=("parallel",), vmem_limit_bytes=VMEM_LIMIT),
        name="peer_v",
    )(idx, w, x2, gfin, tab)


def _t5_bucket(rel):
    nb = REL_BUCKETS // 2
    max_exact = nb // 2
    ret = jnp.where(rel > 0, nb, 0)
    n = jnp.abs(rel)
    large = max_exact + (jnp.log(jnp.maximum(n, 1).astype(F32) / max_exact)
                         / math.log(REL_MAX_DIST / max_exact) * (nb - max_exact)).astype(jnp.int32)
    large = jnp.minimum(large, nb - 1)
    return ret + jnp.where(n < max_exact, n, large)


def _bias_tiles(rel_bias, q_pos, k_pos, k_valid):
    bucket = _t5_bucket(k_pos[None, :] - q_pos[:, None])
    table = rel_bias.astype(F32) * LOG2E
    bias = jnp.zeros((rel_bias.shape[1],) + bucket.shape, F32)
    for bkt in range(REL_BUCKETS):
        bias = jnp.where((bucket == bkt)[None], table[bkt][:, None, None], bias)
    visible = ((k_pos // CHUNK)[None, :] <= (q_pos // CHUNK)[:, None]) & k_valid[None, :]
    return jnp.where(visible[None], bias, NEG_INF)


def _rope_tables(pos):
    half = MLA_ROPE // 2
    inv = ROPE_THETA ** (-jnp.arange(half, dtype=F32) / half)
    ang = pos.astype(F32)[:, None] * inv
    cos, sin = jnp.cos(ang), jnp.sin(ang)
    pad = jnp.zeros((pos.shape[0], LANES - MLA_ROPE), F32)
    return jnp.concatenate([cos, cos, pad], axis=1), jnp.concatenate([-sin, sin, pad], axis=1)


def _pack_table(tab):
    n, d = tab.shape
    bits = lax.bitcast_convert_type(tab.astype(BF16), jnp.uint16).astype(jnp.uint32)
    words = bits[:, :d // 2] | (bits[:, d // 2:] << 16)
    return lax.bitcast_convert_type(words, jnp.int32).reshape(n * PACK_ROWS, LANES)


def _layer_weights(l, norm_mix, w_in, diff_lambda, diff_subln, mla_q_norm, mla_w_uq, mla_kv_norm, mla_w_uk, mla_w_uv,
                   w_branch_a, w_branch_b, w_out, norm_ffn, peer_w_q, peer_keys, peer_u, peer_v):
    d = w_in.shape[1]
    da = DA_HEADS * 2 * DA_HEAD_DIM
    q_lora = mla_w_uq.shape[1]
    kv_lora = mla_w_uk.shape[1]
    nh = mla_w_uq.shape[2]
    win = w_in[l]
    c0 = 3 * da
    cuts = [c0, c0 + q_lora, c0 + q_lora + kv_lora, c0 + q_lora + kv_lora + MLA_ROPE]
    qscale = jnp.concatenate([jnp.full((da,), DA_HEAD_DIM ** -0.5 * LOG2E, F32), jnp.ones((2 * da,), F32)])
    w = {}
    w["gmix"] = norm_mix[l][None]
    w["wqkv"] = (win[:, :c0] * qscale).astype(BF16)
    w["wcq"] = win[:, cuts[0]:cuts[1]].astype(BF16)
    w["wkv"] = jnp.concatenate([win[:, cuts[1]:cuts[3]], jnp.zeros((d, LANES - MLA_ROPE), F32)], axis=1).astype(BF16)
    w["wg"] = win[:, cuts[3]:].astype(BF16)
    w["gq"] = mla_q_norm[l][None]
    w["gkv"] = mla_kv_norm[l][None]
    uq = jnp.transpose(mla_w_uq[l], (1, 0, 2))
    uk = jnp.transpose(mla_w_uk[l], (1, 2, 0))
    wlat = _bmm(uq[:, :, :MLA_NOPE], uk)
    mla_scale = (MLA_NOPE + MLA_ROPE) ** -0.5 * LOG2E
    wqf = jnp.concatenate([wlat, uq[:, :, MLA_NOPE:], jnp.zeros((nh, q_lora, LANES - MLA_ROPE), F32)], axis=2) * mla_scale
    w["wqf"] = jnp.transpose(wqf, (1, 0, 2)).reshape(q_lora, nh * 2 * LANES).astype(BF16)
    w["lam"] = (jnp.exp(jnp.sum(diff_lambda[l][0] * diff_lambda[l][1]))
                - jnp.exp(jnp.sum(diff_lambda[l][2] * diff_lambda[l][3]))).astype(F32)
    w["subln"] = diff_subln[l][None]
    w["wa"] = w_branch_a[l].astype(BF16)
    uv = jnp.transpose(mla_w_uv[l], (1, 0, 2))
    wbb = w_branch_b[l].reshape(nh, uv.shape[2], d)
    w["wb"] = _bmm(uv, wbb).reshape(nh * kv_lora, d).astype(BF16)
    w["wo"] = w_out[l].astype(BF16)
    w["gffn"] = norm_ffn[l][None]
    ng = PEER_HEADS * 2
    dk = peer_keys.shape[-1]
    keys = peer_keys[l].reshape(ng, PEER_N_KEYS, dk)
    wq_t = jnp.transpose(peer_w_q[l]).reshape(ng, dk, d)
    w["wst"] = _bmm(keys, wq_t).reshape(ng * PEER_N_KEYS, d).astype(BF16)
    w["utab"] = _pack_table(peer_u[l])
    w["vtab"] = _pack_table(peer_v[l])
    return w


def _group_forward(x, pos, past, layer_idx, rel_bias, w, gfin, *, tm, tm_post, tq_diff, tq_mla, rs_diff, rs_mla,
                   tm_peer):
    b, s, d = x.shape
    t = b * s
    lambda_init = 0.8 - 0.6 * math.exp(-0.3 * layer_idx)
    cos_t, sin_t = _rope_tables(pos)
    xf = x.reshape(t, d)
    dq, dk, dv, dkb, dvb, qm, ckv, kpe, kf, gate = _in_proj(
        xf, cos_t, sin_t, w["gmix"], w["wqkv"], w["wcq"], w["wkv"], w["wg"], w["gq"], w["gkv"], w["wqf"], seq=s, tm=tm)
    lam = jnp.reshape(w["lam"] + lambda_init, (1,))
    nb = REL_BUCKETS // 2
    cfar = rel_bias[nb - 1].astype(F32) * LOG2E
    if past is None:
        assert s % tq_diff == 0 and s % tq_mla == 0 and tq_diff % CHUNK == 0 and tq_mla % CHUNK == 0
        assert tq_diff >= REL_MAX_DIST
        tq = tq_diff
        q_pos = jnp.arange(tq, 2 * tq, dtype=jnp.int32)
        k_pos = jnp.arange(0, 2 * tq, dtype=jnp.int32)
        both = _bias_tiles(rel_bias, q_pos, k_pos, jnp.ones((2 * tq,), bool))
        bias = jnp.stack([both[:, :, :tq], both[:, :, tq:]], axis=1)
        oa = _diff_attn(dq.reshape(b, s, d), dkb.reshape(b, s, d), dvb.reshape(b, s, d), bias, cfar, lam, w["subln"],
                        tq=tq, tk=tq, rs=rs_diff, causal=True, out_scale=1.0 - lambda_init)
        tqm = tq_mla
        lp = jnp.arange(tqm, dtype=jnp.int32)
        mask = jnp.where((lp // CHUNK)[None, :] <= (lp // CHUNK)[:, None], 0.0, NEG_INF).astype(F32)
        ol = _mla_attn(qm.reshape(b, s, -1), kf.reshape(b, s, -1), mask, tq=tqm, tk=tqm, rs=rs_mla,
                       nh=MLA_HEADS // 2, causal=True)
    else:
        pk, pv, pckv, pkpe = past
        p = pk.shape[1]
        sk = -(-(p + s) // LANES) * LANES
        npad = sk - p - s
        k_pos = jnp.concatenate([jnp.arange(p, dtype=jnp.int32), pos, jnp.zeros((npad,), jnp.int32)])
        k_valid = jnp.arange(sk) < p + s
        zpad = jnp.zeros((b, npad, d), BF16)
        k_all = jnp.concatenate([pk.reshape(b, p, d).astype(BF16), dkb.reshape(b, s, d), zpad], axis=1)
        v_all = jnp.concatenate([pv.reshape(b, p, d).astype(BF16), dvb.reshape(b, s, d), zpad], axis=1)
        bias = _bias_tiles(rel_bias, pos, k_pos, k_valid)[:, None]
        oa = _diff_attn(dq.reshape(b, s, d), k_all, v_all, bias, cfar, lam, w["subln"],
                        tq=s, tk=sk, rs=rs_diff, causal=False, out_scale=1.0 - lambda_init)
        pad = jnp.zeros((b, p, LANES - MLA_ROPE), BF16)
        kf_past = jnp.concatenate([pckv.astype(BF16), pkpe.astype(BF16), pad], axis=2)
        kf_all = jnp.concatenate([kf_past, kf.reshape(b, s, -1), jnp.zeros((b, npad, 2 * LANES), BF16)], axis=1)
        mask = jnp.broadcast_to(jnp.where(k_valid, 0.0, NEG_INF).astype(F32)[None], (s, sk))
        ol = _mla_attn(qm.reshape(b, s, -1), kf_all, mask, tq=s, tk=sk, rs=rs_mla, nh=MLA_HEADS, causal=False)
    x2, h2, st = _post(xf, oa.reshape(t, d), ol.reshape(t, -1), gate, w["wa"], w["wb"], w["wo"], w["gffn"], w["wst"],
                       tm=tm_post)
    idx_t, g_t = _topk(st, tm=tm_peer)
    idx = jnp.transpose(idx_t).reshape(t // IDX_GROUP, IDX_GROUP, -1)
    wts = _peer_u(idx, h2, jnp.transpose(g_t), w["utab"], tm=tm_peer)
    y = _peer_v(idx, wts, x2, gfin, w["vtab"], tm=tm_peer)
    return y.reshape(b, s, d), (dk, dv, ckv, kpe)


def kernel(x_prompt, x_sample, cache_diff_k, cache_diff_v, cache_mla_ckv, cache_mla_kpe, rel_bias, norm_mix, w_in,
           diff_lambda, diff_subln, mla_q_norm, mla_w_uq, mla_kv_norm, mla_w_uk, mla_w_uv, w_branch_a, w_branch_b,
           w_out, norm_ffn, peer_w_q, peer_keys, peer_u, peer_v, norm_final):
    depth = w_in.shape[0]
    assert depth == 1, "the final RMSNorm is fused into the last layer's PEER kernel"
    bp, sp, d = x_prompt.shape
    bs, ss, _ = x_sample.shape
    past_len = cache_diff_k.shape[2]
    pos_p = jnp.arange(sp, dtype=jnp.int32)
    pos_s = past_len + jnp.arange(ss, dtype=jnp.int32)
    gfin = norm_final[None]
    l = 0
    w = _layer_weights(l, norm_mix, w_in, diff_lambda, diff_subln, mla_q_norm, mla_w_uq, mla_kv_norm, mla_w_uk,
                       mla_w_uv, w_branch_a, w_branch_b, w_out, norm_ffn, peer_w_q, peer_keys, peer_u, peer_v)
    yp, (kp, vp, cp, ep) = _group_forward(x_prompt, pos_p, None, l, rel_bias, w, gfin,
                                          tm=min(256, sp), tm_post=256, tq_diff=min(512, sp), tq_mla=min(512, sp),
                                          rs_diff=128, rs_mla=128, tm_peer=128)
    past = (cache_diff_k[l], cache_diff_v[l], cache_mla_ckv[l], cache_mla_kpe[l])
    ys, (ks, vs, cs, es) = _group_forward(x_sample, pos_s, past, l, rel_bias, w, gfin,
                                          tm=ss, tm_post=128, tq_diff=ss, tq_mla=ss, rs_diff=ss, rs_mla=ss,
                                          tm_peer=128)
    nh, hd = DA_HEADS, DA_HEAD_DIM
    return (yp, ys,
            kp.reshape(1, bp, sp, nh, 2, hd), vp.reshape(1, bp, sp, nh, 2 * hd),
            cp.reshape(1, bp, sp, -1), ep.reshape(1, bp, sp, -1),
            ks.reshape(1, bs, ss, nh, 2, hd), vs.reshape(1, bs, ss, nh, 2 * hd),
            cs.reshape(1, bs, ss, -1), es.reshape(1, bs, ss, -1))
```

```python
import functools
import math

import jax
import jax.numpy as jnp
from jax import lax
from jax.experimental import pallas as pl
from jax.experimental.pallas import tpu as pltpu

F32 = jnp.float32
BF16 = jnp.bfloat16

CHUNK = 64
DA_HEADS = 8
DA_HEAD_DIM = 64
MLA_HEADS = 16
MLA_NOPE = 64
MLA_ROPE = 32
ROPE_THETA = 10000.0
REL_BUCKETS = 32
REL_MAX_DIST = 128
PEER_HEADS = 8
PEER_N_KEYS = 128
PEER_TOPK = 16
NORM_EPS = 1e-6
NEG_INF = -1e30
LOG2E = math.log2(math.e)

LANES = 128
PACK_ROWS = 4
TILE_STRIDE = 136
VMEM_LIMIT = 56 * 1024 * 1024
GROUPS_PER_ITER = 2
GATHER_TILES = 4
GATHER_LEAD = 2
IDX_GROUP = 8
IDX_SLOTS = 4


def _const_spec(shape):
    nd = len(shape)
    return pl.BlockSpec(shape, lambda *_: (0,) * nd, pipeline_mode=pl.Buffered(1))


def _rms(x, g):
    return x * lax.rsqrt(jnp.mean(x * x, axis=-1, keepdims=True) + NORM_EPS) * g


def _dot(a, b):
    return jnp.dot(a, b, preferred_element_type=F32)


def _dot_nt(a, b):
    return lax.dot_general(a, b, (((1,), (1,)), ((), ())), preferred_element_type=F32)


def _bmm_kernel(a_ref, b_ref, o_ref):
    o_ref[...] = jnp.dot(a_ref[...], b_ref[...], precision=lax.Precision.HIGHEST, preferred_element_type=F32)


def _bmm(a, b):
    n, m, k = a.shape
    _, _, p = b.shape
    return pl.pallas_call(
        _bmm_kernel,
        grid=(n,),
        in_specs=[pl.BlockSpec((None, m, k), lambda i: (i, 0, 0)), pl.BlockSpec((None, k, p), lambda i: (i, 0, 0))],
        out_specs=pl.BlockSpec((None, m, p), lambda i: (i, 0, 0)),
        out_shape=jax.ShapeDtypeStruct((n, m, p), F32),
        name="fold_bmm",
    )(a, b)


def _in_proj_kernel(x_ref, cos_ref, sin_ref, gmix_ref, wqkv_ref, wcq_ref, wkv_ref, wg_ref, gq_ref, gkv_ref, wqf_ref,
                    dq_ref, dk_ref, dv_ref, dkb_ref, dvb_ref, qm_ref, ckv_ref, kpe_ref, kf_ref, gate_ref, *, d, nh):
    hb = _rms(x_ref[...], gmix_ref[...]).astype(BF16)
    z = _dot(hb, wqkv_ref[...])
    dq_ref[...] = z[:, :d].astype(BF16)
    dk = z[:, d:2 * d]
    dk_ref[...] = dk
    dkb_ref[...] = dk.astype(BF16)
    dv = z[:, 2 * d:]
    dv_ref[...] = dv
    dvb_ref[...] = dv.astype(BF16)

    cosb = cos_ref[...]
    sinb = sin_ref[...]
    lane = lax.broadcasted_iota(jnp.int32, cosb.shape, 1)
    half = MLA_ROPE // 2

    def rope(blk):
        swapped = jnp.where(lane < half, pltpu.roll(blk, LANES - half, 1), pltpu.roll(blk, half, 1))
        return blk * cosb + swapped * sinb

    zkv = _dot(hb, wkv_ref[...])
    ckv = _rms(zkv[:, :LANES], gkv_ref[...])
    ckv_ref[...] = ckv
    kr = rope(zkv[:, LANES:])
    kpe_ref[...] = kr[:, :MLA_ROPE]
    kf_ref[...] = jnp.concatenate([ckv.astype(BF16), kr.astype(BF16)], axis=1)

    cq = _rms(_dot(hb, wcq_ref[...]), gq_ref[...]).astype(BF16)
    qraw = _dot(cq, wqf_ref[...])
    for h in range(nh):
        c0 = 2 * LANES * h
        qm_ref[:, c0:c0 + LANES] = qraw[:, c0:c0 + LANES].astype(BF16)
        qm_ref[:, c0 + LANES:c0 + 2 * LANES] = rope(qraw[:, c0 + LANES:c0 + 2 * LANES]).astype(BF16)

    zg = _dot(hb, wg_ref[...])
    gate_ref[...] = (1.0 / (1.0 + jnp.exp(-zg))).astype(BF16)


def _in_proj(x, cos_t, sin_t, gmix, wqkv, wcq, wkv, wg, gq, gkv, wqf, *, seq, tm):
    t, d = x.shape
    nh = wqf.shape[1] // (2 * LANES)
    nblk = seq // tm
    tok = lambda w: pl.BlockSpec((tm, w), lambda i: (i, 0))
    pos = pl.BlockSpec((tm, LANES), lambda i: (i % nblk, 0))
    outs = [
        (d, BF16), (d, F32), (d, F32), (d, BF16), (d, BF16), (wqf.shape[1], BF16), (LANES, F32), (MLA_ROPE, F32),
        (2 * LANES, BF16), (2 * d, BF16),
    ]
    return pl.pallas_call(
        functools.partial(_in_proj_kernel, d=d, nh=nh),
        grid=(t // tm,),
        in_specs=[tok(d), pos, pos, _const_spec(gmix.shape), _const_spec(wqkv.shape), _const_spec(wcq.shape),
                  _const_spec(wkv.shape), _const_spec(wg.shape), _const_spec(gq.shape), _const_spec(gkv.shape),
                  _const_spec(wqf.shape)],
        out_specs=[tok(w) for w, _ in outs],
        out_shape=[jax.ShapeDtypeStruct((t, w), dt) for w, dt in outs],
        compiler_params=pltpu.CompilerParams(dimension_semantics=("parallel",), vmem_limit_bytes=VMEM_LIMIT),
        name="in_proj",
    )(x, cos_t, sin_t, gmix, wqkv, wcq, wkv, wg, gq, gkv, wqf)


def _attn_pipeline(q_ref, k_at, v_at, n_steps, bias_rows, state, *, rs):
    s_ref, p_ref, m_ref, l_ref, a_ref, acc_ref = state
    rows, tk = p_ref.shape
    nrep = tk // LANES
    ngroups = min(GROUPS_PER_ITER, rows // rs)
    chunk = rs * ngroups
    assert rows % chunk == 0
    m_ref[...] = jnp.full(m_ref.shape, -jnp.inf, F32)
    for ref in (l_ref, a_ref, acc_ref, p_ref):
        ref[...] = jnp.zeros(ref.shape, ref.dtype)
    s_ref[0] = _dot_nt(q_ref[...], k_at(0))

    def pv(r0, j):
        rr = pl.ds(r0, chunk)
        acc_ref[rr, :] = a_ref[rr, :] * acc_ref[rr, :] + _dot(p_ref[rr, :], v_at(j))

    def step(j, c):
        cur = lax.rem(j, 2)
        j_next = jnp.minimum(j + 1, n_steps - 1)
        j_prev = jnp.maximum(j - 1, 0)

        def chunk_iter(r, c2):
            r0 = pl.multiple_of(r * chunk, chunk)
            first = r == 0
            pv(pl.multiple_of(jnp.where(first, rows - chunk, r0 - chunk), chunk), jnp.where(first, j_prev, j))
            for u in range(ngroups):
                g0 = pl.multiple_of(r0 + u * rs, rs)
                s = bias_rows(s_ref[cur, pl.ds(g0, rs), :], j, g0)
                m_old = m_ref[pl.ds(g0, rs), :]
                m_new = jnp.maximum(m_old, jnp.max(s, axis=-1, keepdims=True))
                alpha = jnp.exp2(m_old - m_new)
                p = jnp.exp2(s - jnp.concatenate([m_new] * nrep, axis=1))
                psum = p[:, :LANES]
                for jj in range(1, nrep):
                    psum = psum + p[:, jj * LANES:(jj + 1) * LANES]
                l_ref[pl.ds(g0, rs), :] = alpha * l_ref[pl.ds(g0, rs), :] + psum
                m_ref[pl.ds(g0, rs), :] = m_new
                a_ref[pl.ds(g0, rs), :] = alpha
                p_ref[pl.ds(g0, rs), :] = p.astype(BF16)
            s_ref[1 - cur, pl.ds(r0, chunk), :] = _dot_nt(q_ref[pl.ds(r0, chunk), :], k_at(j_next))
            return c2

        lax.fori_loop(0, rows // chunk, chunk_iter, 0)
        return c

    lax.fori_loop(0, n_steps, step, 0)
    pv(rows - chunk, n_steps - 1)
    return acc_ref[...] / jnp.sum(l_ref[...], axis=-1, keepdims=True)


def _attn_state_shapes(rows, tk):
    return [pltpu.VMEM((2, rows, tk), F32), pltpu.VMEM((rows, tk), BF16), pltpu.VMEM((rows, LANES), F32),
            pltpu.VMEM((rows, LANES), F32), pltpu.VMEM((rows, LANES), F32), pltpu.VMEM((rows, LANES), F32)]


def _diff_attn_kernel(lam_ref, q_ref, k_ref, v_ref, bias_ref, g_ref, o_ref, q2_ref, *state,
                      tq, tk, rs, causal, out_scale):
    i = pl.program_id(2)
    q = q_ref[...]
    lane = lax.broadcasted_iota(jnp.int32, q.shape, 1)
    zero = jnp.zeros_like(q)
    q2_ref[:tq] = jnp.where(lane < DA_HEAD_DIM, q, zero)
    q2_ref[tq:] = jnp.where(lane >= DA_HEAD_DIM, q, zero)
    n_near = bias_ref.shape[0]

    def bias_rows(s, j, g0):
        n = jnp.clip(j - (i + 1 - n_near), 0, n_near - 1) if causal else 0
        b0 = pl.multiple_of(g0 - jnp.where(g0 >= tq, tq, 0), rs)
        return s + bias_ref[n, pl.ds(b0, rs), :]

    def k_at(j):
        return k_ref[pl.ds(pl.multiple_of(j * tk, tk), tk), :]

    def v_at(j):
        return v_ref[pl.ds(pl.multiple_of(j * tk, tk), tk), :]

    o = _attn_pipeline(q2_ref, k_at, v_at, i + 1 if causal else 1, bias_rows, state, rs=rs)
    od = o[:tq] - lam_ref[0] * o[tq:]
    o_ref[...] = (_rms(od, g_ref[...]) * out_scale).astype(BF16)


def _diff_attn(q, k, v, bias, lam, g, *, tq, tk, rs, causal, out_scale):
    b, sq, d = q.shape
    sk = k.shape[1]
    nh = d // LANES
    n_near = bias.shape[1]
    assert tq % rs == 0 and tk % LANES == 0 and sk % tk == 0
    smem = pl.BlockSpec(memory_space=pltpu.SMEM)
    return pl.pallas_call(
        functools.partial(_diff_attn_kernel, tq=tq, tk=tk, rs=rs, causal=causal, out_scale=out_scale),
        grid=(b, nh, sq // tq),
        in_specs=[smem,
                  pl.BlockSpec((None, tq, LANES), lambda bb, hh, ii: (bb, ii, hh)),
                  pl.BlockSpec((None, sk, LANES), lambda bb, hh, ii: (bb, 0, hh)),
                  pl.BlockSpec((None, sk, LANES), lambda bb, hh, ii: (bb, 0, hh)),
                  pl.BlockSpec((None, n_near, tq, bias.shape[3]), lambda bb, hh, ii: (hh, 0, 0, 0)),
                  pl.BlockSpec((1, LANES), lambda bb, hh, ii: (0, 0))],
        out_specs=pl.BlockSpec((None, tq, LANES), lambda bb, hh, ii: (bb, ii, hh)),
        out_shape=jax.ShapeDtypeStruct((b, sq, d), BF16),
        scratch_shapes=[pltpu.VMEM((2 * tq, LANES), BF16)] + _attn_state_shapes(2 * tq, tk),
        compiler_params=pltpu.CompilerParams(dimension_semantics=("parallel", "parallel", "parallel"),
                                             vmem_limit_bytes=VMEM_LIMIT),
        name="diff_attn",
    )(lam, q, k, v, bias, g)


def _mla_attn_kernel(q_ref, kf_ref, mask_ref, o_ref, qs_ref, *state, tq, tk, rs, nh, causal):
    i = pl.program_id(2)
    for h in range(nh):
        qs_ref[h * tq:(h + 1) * tq, :] = q_ref[:, 2 * LANES * h:2 * LANES * (h + 1)]
    last = i if causal else 0

    def bias_rows(s, j, g0):
        n = jnp.where(j == last, 1, 0)
        return s + mask_ref[n, pl.ds(pl.multiple_of(lax.rem(g0, tq), rs), rs), :]

    def k_at(j):
        return kf_ref[pl.ds(pl.multiple_of(j * tk, tk), tk), :]

    def v_at(j):
        return kf_ref[pl.ds(pl.multiple_of(j * tk, tk), tk), :LANES]

    o = _attn_pipeline(qs_ref, k_at, v_at, last + 1, bias_rows, state, rs=rs)
    for h in range(nh):
        o_ref[:, LANES * h:LANES * (h + 1)] = o[h * tq:(h + 1) * tq].astype(BF16)


def _mla_attn(q, kf, mask, *, tq, tk, rs, nh, causal):
    b, sq, w = q.shape
    sk = kf.shape[1]
    ngrp = w // (2 * LANES * nh)
    assert tq % rs == 0 and tk % LANES == 0 and sk % tk == 0 and mask.shape == (2, tq, tk)
    return pl.pallas_call(
        functools.partial(_mla_attn_kernel, tq=tq, tk=tk, rs=rs, nh=nh, causal=causal),
        grid=(b, ngrp, sq // tq),
        in_specs=[pl.BlockSpec((None, tq, nh * 2 * LANES), lambda bb, gg, ii: (bb, ii, gg)),
                  pl.BlockSpec((None, sk, 2 * LANES), lambda bb, gg, ii: (bb, 0, 0)),
                  pl.BlockSpec(mask.shape, lambda bb, gg, ii: (0, 0, 0))],
        out_specs=pl.BlockSpec((None, tq, nh * LANES), lambda bb, gg, ii: (bb, ii, gg)),
        out_shape=jax.ShapeDtypeStruct((b, sq, ngrp * nh * LANES), BF16),
        scratch_shapes=[pltpu.VMEM((nh * tq, 2 * LANES), BF16)] + _attn_state_shapes(nh * tq, tk),
        compiler_params=pltpu.CompilerParams(dimension_semantics=("parallel", "parallel", "parallel"),
                                             vmem_limit_bytes=VMEM_LIMIT),
        name="mla_attn",
    )(q, kf, mask)


def _post_kernel(x_ref, oa_ref, ol_ref, gate_ref, wa_ref, wb_ref, wo_ref, gffn_ref, wst_ref, x2_ref, h2_ref, st_ref,
                 *, d):
    a = _dot(oa_ref[...], wa_ref[...])
    b = _dot(ol_ref[...], wb_ref[...])
    g = gate_ref[...].astype(F32)
    merged = g[:, :d] * a + g[:, d:] * b
    x2 = x_ref[...] + _dot(merged.astype(BF16), wo_ref[...])
    x2_ref[...] = x2
    h2 = _rms(x2, gffn_ref[...])
    h2_ref[...] = h2
    st_ref[...] = _dot_nt(wst_ref[...], h2.astype(BF16))


def _post(x, oa, ol, gate, wa, wb, wo, gffn, wst, *, tm):
    t, d = x.shape
    ns = wst.shape[0]
    tok = lambda w: pl.BlockSpec((tm, w), lambda i: (i, 0))
    return pl.pallas_call(
        functools.partial(_post_kernel, d=d),
        grid=(t // tm,),
        in_specs=[tok(d), tok(oa.shape[1]), tok(ol.shape[1]), tok(gate.shape[1]), _const_spec(wa.shape),
                  _const_spec(wb.shape), _const_spec(wo.shape), _const_spec(gffn.shape), _const_spec(wst.shape)],
        out_specs=[tok(d), tok(d), pl.BlockSpec((ns, tm), lambda i: (0, i))],
        out_shape=[jax.ShapeDtypeStruct((t, d), F32), jax.ShapeDtypeStruct((t, d), F32),
                   jax.ShapeDtypeStruct((ns, t), F32)],
        compiler_params=pltpu.CompilerParams(dimension_semantics=("parallel",), vmem_limit_bytes=VMEM_LIMIT),
        name="post",
    )(x, oa, ol, gate, wa, wb, wo, gffn, wst)


def _top_rows(s, order=None, payload=None):
    if order is None:
        order = lax.broadcasted_iota(jnp.int32, s.shape, 0)
    big = jnp.int32(2 ** 30)
    vals, picks = [], []
    for _ in range(PEER_TOPK):
        m = jnp.max(s, axis=0, keepdims=True)
        pos = jnp.min(jnp.where(s == m, order, big), axis=0, keepdims=True)
        sel = order == pos
        vals.append(m)
        picks.append(pos if payload is None else jnp.sum(jnp.where(sel, payload, 0), axis=0, keepdims=True))
        s = jnp.where(sel, -jnp.inf, s)
    return jnp.concatenate(vals, axis=0), jnp.concatenate(picks, axis=0)


def _pair_candidates(va, ia, vb, ib):
    k, nk = PEER_TOPK, PEER_N_KEYS
    split = 4
    unused = jnp.int32(2 ** 29)
    vals, orders, idxs = [], [], []
    for a in range(split):
        n = k // (a + 1)
        rows = -(-n // 8) * 8
        r = lax.broadcasted_iota(jnp.int32, (rows, va.shape[1]), 0)
        valid = r < n
        vals.append(jnp.where(valid, va[a:a + 1] + vb[:rows], -jnp.inf))
        orders.append(jnp.where(valid, a * k + r, unused))
        idxs.append(ia[a:a + 1] * nk + ib[:rows])
    for b in range(k // (split + 1)):
        n = k // (b + 1)
        rows = -(-n // 8) * 8
        r = lax.broadcasted_iota(jnp.int32, (rows, va.shape[1]), 0)
        valid = (r >= split) & (r < n)
        vals.append(jnp.where(valid, va[:rows] + vb[b:b + 1], -jnp.inf))
        orders.append(jnp.where(valid, r * k + b, unused))
        idxs.append(ia[:rows] * nk + ib[b:b + 1])
    return jnp.concatenate(vals, axis=0), jnp.concatenate(orders, axis=0), jnp.concatenate(idxs, axis=0)


def _topk_kernel(st_ref, idx_ref, g_ref):
    nk = PEER_N_KEYS

    def head(h, c):
        base = pl.multiple_of(h * 2 * nk, 2 * nk)
        va, ia = _top_rows(st_ref[pl.ds(base, nk), :])
        vb, ib = _top_rows(st_ref[pl.ds(base + nk, nk), :])
        cand, order, cidx = _pair_candidates(va, ia, vb, ib)
        tv, ti = _top_rows(cand, order, cidx)
        e = jnp.exp(tv - tv[0:1])
        ob = pl.multiple_of(h * PEER_TOPK, PEER_TOPK)
        idx_ref[pl.ds(ob, PEER_TOPK), :] = ti * PACK_ROWS
        g_ref[pl.ds(ob, PEER_TOPK), :] = e / jnp.sum(e, axis=0, keepdims=True)
        return c

    lax.fori_loop(0, PEER_HEADS, head, 0)


def _topk(st, *, tm):
    ns, t = st.shape
    nsel = PEER_HEADS * PEER_TOPK
    return pl.pallas_call(
        _topk_kernel,
        grid=(t // tm,),
        in_specs=[pl.BlockSpec((ns, tm), lambda i: (0, i))],
        out_specs=[pl.BlockSpec((nsel, tm), lambda i: (0, i)), pl.BlockSpec((nsel, tm), lambda i: (0, i))],
        out_shape=[jax.ShapeDtypeStruct((nsel, t), jnp.int32), jax.ShapeDtypeStruct((nsel, t), F32)],
        compiler_params=pltpu.CompilerParams(dimension_semantics=("parallel",), vmem_limit_bytes=VMEM_LIMIT),
        name="topk",
    )(st)


def _gather_scratch(nsel):
    return ([pltpu.SMEM((IDX_SLOTS, IDX_GROUP, nsel), jnp.int32), pltpu.SemaphoreType.DMA((IDX_SLOTS,))]
            + [pltpu.VMEM((PACK_ROWS * TILE_STRIDE, LANES), jnp.int32) for _ in range(GATHER_TILES)])


def _tile_rows(tile_ref, nsel):
    x = jnp.concatenate([tile_ref[q * TILE_STRIDE:q * TILE_STRIDE + nsel, :] for q in range(PACK_ROWS)], axis=1)
    return pltpu.bitcast(x, BF16)


def _pipelined_tokens(tm, nsel, idx_hbm, idx_s, sems, tab_ref, tiles, compute):
    n = len(tiles)
    span = IDX_SLOTS * IDX_GROUP
    groups = tm // IDX_GROUP
    assert tm % span == 0 and span % n == 0 and GATHER_LEAD < min(n, IDX_GROUP)
    g0 = pl.program_id(0) * groups

    def idx_copy(group, slot):
        g = g0 + jnp.minimum(group, groups - 1)
        return pltpu.make_async_copy(idx_hbm.at[g], idx_s.at[slot], sems.at[slot])

    def gather(pos, tile_ref):
        slot, row = (pos // IDX_GROUP) % IDX_SLOTS, pos % IDX_GROUP
        for mi in range(nsel):
            i = pl.multiple_of(idx_s[slot, row, mi], PACK_ROWS)
            tile_ref[pl.ds(mi, PACK_ROWS, stride=TILE_STRIDE), :] = tab_ref[pl.ds(i, PACK_ROWS), :]

    for s in range(IDX_SLOTS):
        idx_copy(s, s).start()
    idx_copy(0, 0).wait()
    for j in range(GATHER_LEAD):
        gather(j, tiles[j % n])

    def body(b, c):
        for j in range(span):
            pos = j + GATHER_LEAD
            slot = (pos // IDX_GROUP) % IDX_SLOTS
            group = b * IDX_SLOTS + pos // IDX_GROUP
            if pos % IDX_GROUP == 0:
                idx_copy(group, slot).wait()
            gather(pos, tiles[pos % n])
            if pos % IDX_GROUP == IDX_GROUP - 1:
                idx_copy(group + IDX_SLOTS, slot).start()
            compute(b * span + j, tiles[j % n])
        return c

    lax.fori_loop(0, tm // span, body, 0)
    for s in range(1, IDX_SLOTS):
        idx_copy(0, s).wait()


def _two_rows(lo, hi):
    row = lax.broadcasted_iota(jnp.int32, (8, lo.shape[1]), 0)
    return jnp.where(row < 4, jnp.broadcast_to(lo, (8, lo.shape[1])), jnp.broadcast_to(hi, (8, hi.shape[1]))).astype(BF16)


def _peer_u_kernel(idx_hbm, h_ref, g_ref, tab_ref, w_ref, z_ref, idx_s, sems, *tiles, tm, nsel):
    half = h_ref.shape[1] // 2

    def compute(t, tile_ref):
        hrow = h_ref[pl.ds(t, 1), :]
        z8 = _dot_nt(_two_rows(hrow[:, :half], hrow[:, half:]), _tile_rows(tile_ref, nsel))
        lane = lax.broadcasted_iota(jnp.int32, (1, 2 * nsel), 1)
        z_ref[pl.ds(t, 1), :] = jnp.where(lane % 2 == 0, z8[0:1], z8[4:5])

    _pipelined_tokens(tm, nsel, idx_hbm, idx_s, sems, tab_ref, tiles, compute)
    r = lax.broadcasted_iota(jnp.int32, (2 * nsel, nsel), 0)
    col = lax.broadcasted_iota(jnp.int32, (2 * nsel, nsel), 1)
    pair = (r // 2 == col).astype(F32)
    act = jnp.dot(z_ref[...], pair, precision=lax.Precision.HIGHEST, preferred_element_type=F32)
    gelu = 0.5 * act * (1.0 + lax.erf(act * (1.0 / math.sqrt(2.0))))
    w_ref[...] = g_ref[...] * gelu


def _peer_u(idx, h2, g, tab, *, tm):
    t, d = h2.shape
    nsel = idx.shape[2]
    tok = lambda w: pl.BlockSpec((tm, w), lambda i: (i, 0))
    return pl.pallas_call(
        functools.partial(_peer_u_kernel, tm=tm, nsel=nsel),
        grid=(t // tm,),
        in_specs=[pl.BlockSpec(memory_space=pl.ANY), tok(d), tok(nsel), _const_spec(tab.shape)],
        out_specs=tok(nsel),
        out_shape=jax.ShapeDtypeStruct((t, nsel), F32),
        scratch_shapes=[pltpu.VMEM((tm, 2 * nsel), F32)] + _gather_scratch(nsel),
        compiler_params=pltpu.CompilerParams(dimension_semantics=("parallel",), vmem_limit_bytes=VMEM_LIMIT),
        name="peer_u",
    )(idx, h2, g, tab)


def _peer_v_kernel(idx_hbm, w_ref, x2_ref, gfin_ref, tab_ref, y_ref, we_ref, wo_ref, po_ref, idx_s, sems, *tiles,
                   tm, nsel):
    wb = w_ref[...].astype(BF16)
    r = lax.broadcasted_iota(jnp.int32, (nsel, 2 * nsel), 0)
    col = lax.broadcasted_iota(jnp.int32, (nsel, 2 * nsel), 1)
    we_ref[...] = _dot(wb, (col == 2 * r).astype(BF16))
    wo_ref[...] = _dot(wb, (col == 2 * r + 1).astype(BF16))

    def compute(t, tile_ref):
        o8 = _dot(_two_rows(we_ref[pl.ds(t, 1), :], wo_ref[pl.ds(t, 1), :]), _tile_rows(tile_ref, nsel))
        po_ref[pl.ds(t, 1), :] = jnp.concatenate([o8[0:1], o8[4:5]], axis=1)

    _pipelined_tokens(tm, nsel, idx_hbm, idx_s, sems, tab_ref, tiles, compute)
    y_ref[...] = _rms(x2_ref[...] + po_ref[...], gfin_ref[...])


def _peer_v(idx, w, x2, gfin, tab, *, tm):
    t, d = x2.shape
    nsel = idx.shape[2]
    tok = lambda wd: pl.BlockSpec((tm, wd), lambda i: (i, 0))
    return pl.pallas_call(
        functools.partial(_peer_v_kernel, tm=tm, nsel=nsel),
        grid=(t // tm,),
        in_specs=[pl.BlockSpec(memory_space=pl.ANY), tok(nsel), tok(d), _const_spec(gfin.shape),
                  _const_spec(tab.shape)],
        out_specs=tok(d),
        out_shape=jax.ShapeDtypeStruct((t, d), F32),
        scratch_shapes=[pltpu.VMEM((tm, 2 * nsel), F32), pltpu.VMEM((tm, 2 * nsel), F32), pltpu.VMEM((tm, d), F32)]
        + _gather_scratch(nsel),
        compiler_params=pltpu.CompilerParams(dimension_semantics=("parallel",), vmem_limit_bytes=VMEM_LIMIT),
        name="peer_v",
    )(idx, w, x2, gfin, tab)


def _t5_bucket(rel):
    nb = REL_BUCKETS // 2
    max_exact = nb // 2
    ret = jnp.where(rel > 0, nb, 0)
    n = jnp.abs(rel)
    large = max_exact + (jnp.log(jnp.maximum(n, 1).astype(F32) / max_exact)
                         / math.log(REL_MAX_DIST / max_exact) * (nb - max_exact)).astype(jnp.int32)
    large = jnp.minimum(large, nb - 1)
    return ret + jnp.where(n < max_exact, n, large)


def _bias_tiles(rel_bias, q_pos, k_pos, k_valid):
    bucket = _t5_bucket(k_pos[None, :] - q_pos[:, None])
    table = rel_bias.astype(F32) * LOG2E
    bias = jnp.zeros((rel_bias.shape[1],) + bucket.shape, F32)
    for bkt in range(REL_BUCKETS):
        bias = jnp.where((bucket == bkt)[None], table[bkt][:, None, None], bias)
    visible = ((k_pos // CHUNK)[None, :] <= (q_pos // CHUNK)[:, None]) & k_valid[None, :]
    return jnp.where(visible[None], bias, NEG_INF)


def _rope_tables(pos):
    half = MLA_ROPE // 2
    inv = ROPE_THETA ** (-jnp.arange(half, dtype=F32) / half)
    ang = pos.astype(F32)[:, None] * inv
    cos, sin = jnp.cos(ang), jnp.sin(ang)
    pad = jnp.zeros((pos.shape[0], LANES - MLA_ROPE), F32)
    return jnp.concatenate([cos, cos, pad], axis=1), jnp.concatenate([-sin, sin, pad], axis=1)


def _pack_table(tab):
    n, d = tab.shape
    bits = lax.bitcast_convert_type(tab.astype(BF16), jnp.uint16).astype(jnp.uint32)
    words = bits[:, :d // 2] | (bits[:, d // 2:] << 16)
    return lax.bitcast_convert_type(words, jnp.int32).reshape(n * PACK_ROWS, LANES)


def _layer_weights(l, norm_mix, w_in, diff_lambda, diff_subln, mla_q_norm, mla_w_uq, mla_kv_norm, mla_w_uk, mla_w_uv,
                   w_branch_a, w_branch_b, w_out, norm_ffn, peer_w_q, peer_keys, peer_u, peer_v):
    d = w_in.shape[1]
    da = DA_HEADS * 2 * DA_HEAD_DIM
    q_lora = mla_w_uq.shape[1]
    kv_lora = mla_w_uk.shape[1]
    nh = mla_w_uq.shape[2]
    win = w_in[l]
    c0 = 3 * da
    cuts = [c0, c0 + q_lora, c0 + q_lora + kv_lora, c0 + q_lora + kv_lora + MLA_ROPE]
    qscale = jnp.concatenate([jnp.full((da,), DA_HEAD_DIM ** -0.5 * LOG2E, F32), jnp.ones((2 * da,), F32)])
    w = {}
    w["gmix"] = norm_mix[l][None]
    w["wqkv"] = (win[:, :c0] * qscale).astype(BF16)
    w["wcq"] = win[:, cuts[0]:cuts[1]].astype(BF16)
    w["wkv"] = jnp.concatenate([win[:, cuts[1]:cuts[3]], jnp.zeros((d, LANES - MLA_ROPE), F32)], axis=1).astype(BF16)
    w["wg"] = win[:, cuts[3]:].astype(BF16)
    w["gq"] = mla_q_norm[l][None]
    w["gkv"] = mla_kv_norm[l][None]
    uq = jnp.transpose(mla_w_uq[l], (1, 0, 2))
    uk = jnp.transpose(mla_w_uk[l], (1, 2, 0))
    wlat = _bmm(uq[:, :, :MLA_NOPE], uk)
    mla_scale = (MLA_NOPE + MLA_ROPE) ** -0.5 * LOG2E
    wqf = jnp.concatenate([wlat, uq[:, :, MLA_NOPE:], jnp.zeros((nh, q_lora, LANES - MLA_ROPE), F32)], axis=2) * mla_scale
    w["wqf"] = jnp.transpose(wqf, (1, 0, 2)).reshape(q_lora, nh * 2 * LANES).astype(BF16)
    w["lam"] = (jnp.exp(jnp.sum(diff_lambda[l][0] * diff_lambda[l][1]))
                - jnp.exp(jnp.sum(diff_lambda[l][2] * diff_lambda[l][3]))).astype(F32)
    w["subln"] = diff_subln[l][None]
    w["wa"] = w_branch_a[l].astype(BF16)
    uv = jnp.transpose(mla_w_uv[l], (1, 0, 2))
    wbb = w_branch_b[l].reshape(nh, uv.shape[2], d)
    w["wb"] = _bmm(uv, wbb).reshape(nh * kv_lora, d).astype(BF16)
    w["wo"] = w_out[l].astype(BF16)
    w["gffn"] = norm_ffn[l][None]
    ng = PEER_HEADS * 2
    dk = peer_keys.shape[-1]
    keys = peer_keys[l].reshape(ng, PEER_N_KEYS, dk)
    wq_t = jnp.transpose(peer_w_q[l]).reshape(ng, dk, d)
    w["wst"] = _bmm(keys, wq_t).reshape(ng * PEER_N_KEYS, d).astype(BF16)
    w["utab"] = _pack_table(peer_u[l])
    w["vtab"] = _pack_table(peer_v[l])
    return w


def _group_forward(x, pos, past, layer_idx, rel_bias, w, gfin, *, tm, tm_post, tq_diff, tq_mla, rs_diff, rs_mla,
                   tm_peer):
    b, s, d = x.shape
    t = b * s
    lambda_init = 0.8 - 0.6 * math.exp(-0.3 * layer_idx)
    cos_t, sin_t = _rope_tables(pos)
    xf = x.reshape(t, d)
    dq, dk, dv, dkb, dvb, qm, ckv, kpe, kf, gate = _in_proj(
        xf, cos_t, sin_t, w["gmix"], w["wqkv"], w["wcq"], w["wkv"], w["wg"], w["gq"], w["gkv"], w["wqf"], seq=s, tm=tm)
    lam = jnp.reshape(w["lam"] + lambda_init, (1,))
    nb = REL_BUCKETS // 2
    cfar = rel_bias[nb - 1].astype(F32) * LOG2E
    if past is None:
        assert s % tq_diff == 0 and s % tq_mla == 0 and tq_diff % CHUNK == 0 and tq_mla % CHUNK == 0
        assert tq_diff >= REL_MAX_DIST
        tq = tq_diff
        q_pos = jnp.arange(tq, 2 * tq, dtype=jnp.int32)
        k_pos = jnp.arange(0, 2 * tq, dtype=jnp.int32)
        both = _bias_tiles(rel_bias, q_pos, k_pos, jnp.ones((2 * tq,), bool))
        far = jnp.broadcast_to(cfar[:, None, None], (cfar.shape[0], tq, tq))
        bias = jnp.stack([far, both[:, :, :tq], both[:, :, tq:]], axis=1)
        oa = _diff_attn(dq.reshape(b, s, d), dkb.reshape(b, s, d), dvb.reshape(b, s, d), bias, lam, w["subln"],
                        tq=tq, tk=tq, rs=rs_diff, causal=True, out_scale=1.0 - lambda_init)
        tqm = tq_mla
        lp = jnp.arange(tqm, dtype=jnp.int32)
        diag = jnp.where((lp // CHUNK)[None, :] <= (lp // CHUNK)[:, None], 0.0, NEG_INF).astype(F32)
        mask = jnp.stack([jnp.zeros_like(diag), diag])
        ol = _mla_attn(qm.reshape(b, s, -1), kf.reshape(b, s, -1), mask, tq=tqm, tk=tqm, rs=rs_mla,
                       nh=MLA_HEADS // 2, causal=True)
    else:
        pk, pv, pckv, pkpe = past
        p = pk.shape[1]
        sk = -(-(p + s) // LANES) * LANES
        npad = sk - p - s
        k_pos = jnp.concatenate([jnp.arange(p, dtype=jnp.int32), pos, jnp.zeros((npad,), jnp.int32)])
        k_valid = jnp.arange(sk) < p + s
        zpad = jnp.zeros((b, npad, d), BF16)
        k_all = jnp.concatenate([pk.reshape(b, p, d).astype(BF16), dkb.reshape(b, s, d), zpad], axis=1)
        v_all = jnp.concatenate([pv.reshape(b, p, d).astype(BF16), dvb.reshape(b, s, d), zpad], axis=1)
        bias = _bias_tiles(rel_bias, pos, k_pos, k_valid)[:, None]
        oa = _diff_attn(dq.reshape(b, s, d), k_all, v_all, bias, lam, w["subln"],
                        tq=s, tk=sk, rs=rs_diff, causal=False, out_scale=1.0 - lambda_init)
        pad = jnp.zeros((b, p, LANES - MLA_ROPE), BF16)
        kf_past = jnp.concatenate([pckv.astype(BF16), pkpe.astype(BF16), pad], axis=2)
        kf_all = jnp.concatenate([kf_past, kf.reshape(b, s, -1), jnp.zeros((b, npad, 2 * LANES), BF16)], axis=1)
        padded = jnp.broadcast_to(jnp.where(k_valid, 0.0, NEG_INF).astype(F32)[None], (s, sk))
        mask = jnp.stack([jnp.zeros_like(padded), padded])
        ol = _mla_attn(qm.reshape(b, s, -1), kf_all, mask, tq=s, tk=sk, rs=rs_mla, nh=MLA_HEADS, causal=False)
    x2, h2, st = _post(xf, oa.reshape(t, d), ol.reshape(t, -1), gate, w["wa"], w["wb"], w["wo"], w["gffn"], w["wst"],
                       tm=tm_post)
    idx_t, g_t = _topk(st, tm=tm_peer)
    idx = jnp.transpose(idx_t).reshape(t // IDX_GROUP, IDX_GROUP, -1)
    wts = _peer_u(idx, h2, jnp.transpose(g_t), w["utab"], tm=tm_peer)
    y = _peer_v(idx, wts, x2, gfin, w["vtab"], tm=tm_peer)
    return y.reshape(b, s, d), (dk, dv, ckv, kpe)


def kernel(x_prompt, x_sample, cache_diff_k, cache_diff_v, cache_mla_ckv, cache_mla_kpe, rel_bias, norm_mix, w_in,
           diff_lambda, diff_subln, mla_q_norm, mla_w_uq, mla_kv_norm, mla_w_uk, mla_w_uv, w_branch_a, w_branch_b,
           w_out, norm_ffn, peer_w_q, peer_keys, peer_u, peer_v, norm_final):
    depth = w_in.shape[0]
    assert depth == 1, "the final RMSNorm is fused into the last layer's PEER kernel"
    bp, sp, d = x_prompt.shape
    bs, ss, _ = x_sample.shape
    past_len = cache_diff_k.shape[2]
    pos_p = jnp.arange(sp, dtype=jnp.int32)
    pos_s = past_len + jnp.arange(ss, dtype=jnp.int32)
    gfin = norm_final[None]
    l = 0
    w = _layer_weights(l, norm_mix, w_in, diff_lambda, diff_subln, mla_q_norm, mla_w_uq, mla_kv_norm, mla_w_uk,
                       mla_w_uv, w_branch_a, w_branch_b, w_out, norm_ffn, peer_w_q, peer_keys, peer_u, peer_v)
    yp, (kp, vp, cp, ep) = _group_forward(x_prompt, pos_p, None, l, rel_bias, w, gfin,
                                          tm=min(256, sp), tm_post=256, tq_diff=min(512, sp), tq_mla=min(512, sp),
                                          rs_diff=128, rs_mla=128, tm_peer=128)
    past = (cache_diff_k[l], cache_diff_v[l], cache_mla_ckv[l], cache_mla_kpe[l])
    ys, (ks, vs, cs, es) = _group_forward(x_sample, pos_s, past, l, rel_bias, w, gfin,
                                          tm=ss, tm_post=128, tq_diff=ss, tq_mla=ss, rs_diff=ss, rs_mla=ss,
                                          tm_peer=128)
    nh, hd = DA_HEADS, DA_HEAD_DIM
    return (yp, ys,
            kp.reshape(1, bp, sp, nh, 2, hd), vp.reshape(1, bp, sp, nh, 2 * hd),
            cp.reshape(1, bp, sp, -1), ep.reshape(1, bp, sp, -1),
            ks.reshape(1, bs, ss, nh, 2, hd), vs.reshape(1, bs, ss, nh, 2 * hd),
            cs.reshape(1, bs, ss, -1), es.reshape(1, bs, ss, -1))
```

```python
import functools
import math

import jax
import jax.numpy as jnp
from jax import lax
from jax.experimental import pallas as pl
from jax.experimental.pallas import tpu as pltpu

F32 = jnp.float32
BF16 = jnp.bfloat16

CHUNK = 64
DA_HEADS = 8
DA_HEAD_DIM = 64
MLA_HEADS = 16
MLA_NOPE = 64
MLA_ROPE = 32
ROPE_THETA = 10000.0
REL_BUCKETS = 32
REL_MAX_DIST = 128
PEER_HEADS = 8
PEER_N_KEYS = 128
PEER_TOPK = 16
NORM_EPS = 1e-6
NEG_INF = -1e30
LOG2E = math.log2(math.e)

LANES = 128
PACK_ROWS = 4
TILE_STRIDE = 136
VMEM_LIMIT = 56 * 1024 * 1024
GROUPS_PER_ITER = 2
GATHER_TILES = 4
GATHER_LEAD = 2
IDX_GROUP = 8
IDX_SLOTS = 4


def _const_spec(shape):
    nd = len(shape)
    return pl.BlockSpec(shape, lambda *_: (0,) * nd, pipeline_mode=pl.Buffered(1))


def _rms(x, g):
    return x * lax.rsqrt(jnp.mean(x * x, axis=-1, keepdims=True) + NORM_EPS) * g


def _dot(a, b):
    return jnp.dot(a, b, preferred_element_type=F32)


def _dot_nt(a, b):
    return lax.dot_general(a, b, (((1,), (1,)), ((), ())), preferred_element_type=F32)


def _bmm_kernel(a_ref, b_ref, o_ref):
    o_ref[...] = jnp.dot(a_ref[...], b_ref[...], precision=lax.Precision.HIGHEST, preferred_element_type=F32)


def _bmm(a, b):
    n, m, k = a.shape
    _, _, p = b.shape
    return pl.pallas_call(
        _bmm_kernel,
        grid=(n,),
        in_specs=[pl.BlockSpec((None, m, k), lambda i: (i, 0, 0)), pl.BlockSpec((None, k, p), lambda i: (i, 0, 0))],
        out_specs=pl.BlockSpec((None, m, p), lambda i: (i, 0, 0)),
        out_shape=jax.ShapeDtypeStruct((n, m, p), F32),
        name="fold_bmm",
    )(a, b)


def _in_proj_kernel(x_ref, cos_ref, sin_ref, gmix_ref, wqkv_ref, wcq_ref, wkv_ref, wg_ref, gq_ref, gkv_ref, wqf_ref,
                    dq_ref, dk_ref, dv_ref, dkb_ref, dvb_ref, qm_ref, ckv_ref, kpe_ref, kf_ref, gate_ref, *, d, nh):
    hb = _rms(x_ref[...], gmix_ref[...]).astype(BF16)
    z = _dot(hb, wqkv_ref[...])
    dq_ref[...] = z[:, :d].astype(BF16)
    dk = z[:, d:2 * d]
    dk_ref[...] = dk
    dkb_ref[...] = dk.astype(BF16)
    dv = z[:, 2 * d:]
    dv_ref[...] = dv
    dvb_ref[...] = dv.astype(BF16)

    cosb = cos_ref[...]
    sinb = sin_ref[...]
    lane = lax.broadcasted_iota(jnp.int32, cosb.shape, 1)
    half = MLA_ROPE // 2

    def rope(blk):
        swapped = jnp.where(lane < half, pltpu.roll(blk, LANES - half, 1), pltpu.roll(blk, half, 1))
        return blk * cosb + swapped * sinb

    zkv = _dot(hb, wkv_ref[...])
    ckv = _rms(zkv[:, :LANES], gkv_ref[...])
    ckv_ref[...] = ckv
    kr = rope(zkv[:, LANES:])
    kpe_ref[...] = kr[:, :MLA_ROPE]
    kf_ref[...] = jnp.concatenate([ckv.astype(BF16), kr.astype(BF16)], axis=1)

    cq = _rms(_dot(hb, wcq_ref[...]), gq_ref[...]).astype(BF16)
    qraw = _dot(cq, wqf_ref[...])
    for h in range(nh):
        c0 = 2 * LANES * h
        qm_ref[:, c0:c0 + LANES] = qraw[:, c0:c0 + LANES].astype(BF16)
        qm_ref[:, c0 + LANES:c0 + 2 * LANES] = rope(qraw[:, c0 + LANES:c0 + 2 * LANES]).astype(BF16)

    zg = _dot(hb, wg_ref[...])
    gate_ref[...] = (1.0 / (1.0 + jnp.exp(-zg))).astype(BF16)


def _in_proj(x, cos_t, sin_t, gmix, wqkv, wcq, wkv, wg, gq, gkv, wqf, *, seq, tm):
    t, d = x.shape
    nh = wqf.shape[1] // (2 * LANES)
    nblk = seq // tm
    tok = lambda w: pl.BlockSpec((tm, w), lambda i: (i, 0))
    pos = pl.BlockSpec((tm, LANES), lambda i: (i % nblk, 0))
    outs = [
        (d, BF16), (d, F32), (d, F32), (d, BF16), (d, BF16), (wqf.shape[1], BF16), (LANES, F32), (MLA_ROPE, F32),
        (2 * LANES, BF16), (2 * d, BF16),
    ]
    return pl.pallas_call(
        functools.partial(_in_proj_kernel, d=d, nh=nh),
        grid=(t // tm,),
        in_specs=[tok(d), pos, pos, _const_spec(gmix.shape), _const_spec(wqkv.shape), _const_spec(wcq.shape),
                  _const_spec(wkv.shape), _const_spec(wg.shape), _const_spec(gq.shape), _const_spec(gkv.shape),
                  _const_spec(wqf.shape)],
        out_specs=[tok(w) for w, _ in outs],
        out_shape=[jax.ShapeDtypeStruct((t, w), dt) for w, dt in outs],
        compiler_params=pltpu.CompilerParams(dimension_semantics=("parallel",), vmem_limit_bytes=VMEM_LIMIT),
        name="in_proj",
    )(x, cos_t, sin_t, gmix, wqkv, wcq, wkv, wg, gq, gkv, wqf)


def _attn_pipeline(q_ref, k_at, v_at, n_steps, bias_rows, state, *, rs):
    s_ref, p_ref, m_ref, l_ref, a_ref, acc_ref = state
    rows, tk = p_ref.shape
    nrep = tk // LANES
    ngroups = min(GROUPS_PER_ITER, rows // rs)
    chunk = rs * ngroups
    assert rows % chunk == 0
    m_ref[...] = jnp.full(m_ref.shape, -jnp.inf, F32)
    for ref in (l_ref, a_ref, acc_ref, p_ref):
        ref[...] = jnp.zeros(ref.shape, ref.dtype)
    s_ref[0] = _dot_nt(q_ref[...], k_at(0))

    def pv(r0, j):
        rr = pl.ds(r0, chunk)
        acc_ref[rr, :] = a_ref[rr, :] * acc_ref[rr, :] + _dot(p_ref[rr, :], v_at(j))

    def step(j, c):
        cur = lax.rem(j, 2)
        j_next = jnp.minimum(j + 1, n_steps - 1)
        j_prev = jnp.maximum(j - 1, 0)

        def chunk_iter(r, c2):
            r0 = pl.multiple_of(r * chunk, chunk)
            first = r == 0
            pv(pl.multiple_of(jnp.where(first, rows - chunk, r0 - chunk), chunk), jnp.where(first, j_prev, j))
            for u in range(ngroups):
                g0 = pl.multiple_of(r0 + u * rs, rs)
                s = bias_rows(s_ref[cur, pl.ds(g0, rs), :], j, g0)
                m_old = m_ref[pl.ds(g0, rs), :]
                m_new = jnp.maximum(m_old, jnp.max(s, axis=-1, keepdims=True))
                alpha = jnp.exp2(m_old - m_new)
                p = jnp.exp2(s - jnp.concatenate([m_new] * nrep, axis=1))
                psum = p[:, :LANES]
                for jj in range(1, nrep):
                    psum = psum + p[:, jj * LANES:(jj + 1) * LANES]
                l_ref[pl.ds(g0, rs), :] = alpha * l_ref[pl.ds(g0, rs), :] + psum
                m_ref[pl.ds(g0, rs), :] = m_new
                a_ref[pl.ds(g0, rs), :] = alpha
                p_ref[pl.ds(g0, rs), :] = p.astype(BF16)
            s_ref[1 - cur, pl.ds(r0, chunk), :] = _dot_nt(q_ref[pl.ds(r0, chunk), :], k_at(j_next))
            return c2

        lax.fori_loop(0, rows // chunk, chunk_iter, 0)
        return c

    lax.fori_loop(0, n_steps, step, 0)
    pv(rows - chunk, n_steps - 1)
    return acc_ref[...] / jnp.sum(l_ref[...], axis=-1, keepdims=True)


def _attn_state_shapes(rows, tk):
    return [pltpu.VMEM((2, rows, tk), F32), pltpu.VMEM((rows, tk), BF16), pltpu.VMEM((rows, LANES), F32),
            pltpu.VMEM((rows, LANES), F32), pltpu.VMEM((rows, LANES), F32), pltpu.VMEM((rows, LANES), F32)]


def _diff_attn_kernel(lam_ref, q_ref, k_ref, v_ref, bias_ref, g_ref, o_ref, q2_ref, *state,
                      tq, tk, rs, causal, out_scale):
    i = pl.program_id(2)
    q = q_ref[...]
    lane = lax.broadcasted_iota(jnp.int32, q.shape, 1)
    zero = jnp.zeros_like(q)
    q2_ref[:tq] = jnp.where(lane < DA_HEAD_DIM, q, zero)
    q2_ref[tq:] = jnp.where(lane >= DA_HEAD_DIM, q, zero)
    n_near = bias_ref.shape[0]

    def bias_rows(s, j, g0):
        n = jnp.clip(j - (i + 1 - n_near), 0, n_near - 1) if causal else 0
        b0 = pl.multiple_of(g0 - jnp.where(g0 >= tq, tq, 0), rs)
        return s + bias_ref[n, pl.ds(b0, rs), :]

    def k_at(j):
        return k_ref[pl.ds(pl.multiple_of(j * tk, tk), tk), :]

    def v_at(j):
        return v_ref[pl.ds(pl.multiple_of(j * tk, tk), tk), :]

    o = _attn_pipeline(q2_ref, k_at, v_at, i + 1 if causal else 1, bias_rows, state, rs=rs)
    od = o[:tq] - lam_ref[0] * o[tq:]
    o_ref[...] = (_rms(od, g_ref[...]) * out_scale).astype(BF16)


def _diff_attn(q, k, v, bias, lam, g, *, tq, tk, rs, causal, out_scale):
    b, sq, d = q.shape
    sk = k.shape[1]
    nh = d // LANES
    n_near = bias.shape[1]
    assert tq % rs == 0 and tk % LANES == 0 and sk % tk == 0
    smem = pl.BlockSpec(memory_space=pltpu.SMEM)
    return pl.pallas_call(
        functools.partial(_diff_attn_kernel, tq=tq, tk=tk, rs=rs, causal=causal, out_scale=out_scale),
        grid=(b, nh, sq // tq),
        in_specs=[smem,
                  pl.BlockSpec((None, tq, LANES), lambda bb, hh, ii: (bb, ii, hh)),
                  pl.BlockSpec((None, sk, LANES), lambda bb, hh, ii: (bb, 0, hh)),
                  pl.BlockSpec((None, sk, LANES), lambda bb, hh, ii: (bb, 0, hh)),
                  pl.BlockSpec((None, n_near, tq, bias.shape[3]), lambda bb, hh, ii: (hh, 0, 0, 0)),
                  pl.BlockSpec((1, LANES), lambda bb, hh, ii: (0, 0))],
        out_specs=pl.BlockSpec((None, tq, LANES), lambda bb, hh, ii: (bb, ii, hh)),
        out_shape=jax.ShapeDtypeStruct((b, sq, d), BF16),
        scratch_shapes=[pltpu.VMEM((2 * tq, LANES), BF16)] + _attn_state_shapes(2 * tq, tk),
        compiler_params=pltpu.CompilerParams(dimension_semantics=("parallel", "parallel", "parallel"),
                                             vmem_limit_bytes=VMEM_LIMIT),
        name="diff_attn",
    )(lam, q, k, v, bias, g)


def _mla_attn_kernel(q_ref, kf_ref, mask_ref, o_ref, qs_ref, *state, tq, tk, rs, nh, causal):
    i = pl.program_id(2)
    for h in range(nh):
        qs_ref[h * tq:(h + 1) * tq, :] = q_ref[:, 2 * LANES * h:2 * LANES * (h + 1)]
    last = i if causal else 0

    def bias_rows(s, j, g0):
        n = jnp.where(j == last, 1, 0)
        return s + mask_ref[n, pl.ds(pl.multiple_of(lax.rem(g0, tq), rs), rs), :]

    def k_at(j):
        return kf_ref[pl.ds(pl.multiple_of(j * tk, tk), tk), :]

    def v_at(j):
        return kf_ref[pl.ds(pl.multiple_of(j * tk, tk), tk), :LANES]

    o = _attn_pipeline(qs_ref, k_at, v_at, last + 1, bias_rows, state, rs=rs)
    for h in range(nh):
        o_ref[:, LANES * h:LANES * (h + 1)] = o[h * tq:(h + 1) * tq].astype(BF16)


def _mla_attn(q, kf, mask, *, tq, tk, rs, nh, causal):
    b, sq, w = q.shape
    sk = kf.shape[1]
    ngrp = w // (2 * LANES * nh)
    assert tq % rs == 0 and tk % LANES == 0 and sk % tk == 0 and mask.shape == (2, tq, tk)
    return pl.pallas_call(
        functools.partial(_mla_attn_kernel, tq=tq, tk=tk, rs=rs, nh=nh, causal=causal),
        grid=(b, ngrp, sq // tq),
        in_specs=[pl.BlockSpec((None, tq, nh * 2 * LANES), lambda bb, gg, ii: (bb, ii, gg)),
                  pl.BlockSpec((None, sk, 2 * LANES), lambda bb, gg, ii: (bb, 0, 0)),
                  pl.BlockSpec(mask.shape, lambda bb, gg, ii: (0, 0, 0))],
        out_specs=pl.BlockSpec((None, tq, nh * LANES), lambda bb, gg, ii: (bb, ii, gg)),
        out_shape=jax.ShapeDtypeStruct((b, sq, ngrp * nh * LANES), BF16),
        scratch_shapes=[pltpu.VMEM((nh * tq, 2 * LANES), BF16)] + _attn_state_shapes(nh * tq, tk),
        compiler_params=pltpu.CompilerParams(dimension_semantics=("parallel", "parallel", "parallel"),
                                             vmem_limit_bytes=VMEM_LIMIT),
        name="mla_attn",
    )(q, kf, mask)


def _post_kernel(x_ref, oa_ref, ol_ref, gate_ref, wa_ref, wb_ref, wo_ref, gffn_ref, wst_ref, x2_ref, h2_ref, st_ref,
                 *, d):
    a = _dot(oa_ref[...], wa_ref[...])
    b = _dot(ol_ref[...], wb_ref[...])
    g = gate_ref[...].astype(F32)
    merged = g[:, :d] * a + g[:, d:] * b
    x2 = x_ref[...] + _dot(merged.astype(BF16), wo_ref[...])
    x2_ref[...] = x2
    h2 = _rms(x2, gffn_ref[...])
    h2_ref[...] = h2
    st_ref[...] = _dot_nt(wst_ref[...], h2.astype(BF16))


def _post(x, oa, ol, gate, wa, wb, wo, gffn, wst, *, tm):
    t, d = x.shape
    ns = wst.shape[0]
    tok = lambda w: pl.BlockSpec((tm, w), lambda i: (i, 0))
    return pl.pallas_call(
        functools.partial(_post_kernel, d=d),
        grid=(t // tm,),
        in_specs=[tok(d), tok(oa.shape[1]), tok(ol.shape[1]), tok(gate.shape[1]), _const_spec(wa.shape),
                  _const_spec(wb.shape), _const_spec(wo.shape), _const_spec(gffn.shape), _const_spec(wst.shape)],
        out_specs=[tok(d), tok(d), pl.BlockSpec((ns, tm), lambda i: (0, i))],
        out_shape=[jax.ShapeDtypeStruct((t, d), F32), jax.ShapeDtypeStruct((t, d), F32),
                   jax.ShapeDtypeStruct((ns, t), F32)],
        compiler_params=pltpu.CompilerParams(dimension_semantics=("parallel",), vmem_limit_bytes=VMEM_LIMIT),
        name="post",
    )(x, oa, ol, gate, wa, wb, wo, gffn, wst)


def _top_rows(s, order=None, payload=None):
    if order is None:
        order = lax.broadcasted_iota(jnp.int32, s.shape, 0)
    big = jnp.int32(2 ** 30)
    vals, picks = [], []
    for _ in range(PEER_TOPK):
        m = jnp.max(s, axis=0, keepdims=True)
        pos = jnp.min(jnp.where(s == m, order, big), axis=0, keepdims=True)
        sel = order == pos
        vals.append(m)
        picks.append(pos if payload is None else jnp.sum(jnp.where(sel, payload, 0), axis=0, keepdims=True))
        s = jnp.where(sel, -jnp.inf, s)
    return jnp.concatenate(vals, axis=0), jnp.concatenate(picks, axis=0)


def _pair_candidates(va, ia, vb, ib):
    k, nk = PEER_TOPK, PEER_N_KEYS
    split = 4
    unused = jnp.int32(2 ** 29)
    vals, orders, idxs = [], [], []
    for a in range(split):
        n = k // (a + 1)
        rows = -(-n // 8) * 8
        r = lax.broadcasted_iota(jnp.int32, (rows, va.shape[1]), 0)
        valid = r < n
        vals.append(jnp.where(valid, va[a:a + 1] + vb[:rows], -jnp.inf))
        orders.append(jnp.where(valid, a * k + r, unused))
        idxs.append(ia[a:a + 1] * nk + ib[:rows])
    for b in range(k // (split + 1)):
        n = k // (b + 1)
        rows = -(-n // 8) * 8
        r = lax.broadcasted_iota(jnp.int32, (rows, va.shape[1]), 0)
        valid = (r >= split) & (r < n)
        vals.append(jnp.where(valid, va[:rows] + vb[b:b + 1], -jnp.inf))
        orders.append(jnp.where(valid, r * k + b, unused))
        idxs.append(ia[:rows] * nk + ib[b:b + 1])
    return jnp.concatenate(vals, axis=0), jnp.concatenate(orders, axis=0), jnp.concatenate(idxs, axis=0)


def _topk_head(st_ref, idx_ref, g_ref, h):
    nk = PEER_N_KEYS
    base = pl.multiple_of(h * 2 * nk, 2 * nk)
    va, ia = _top_rows(st_ref[pl.ds(base, nk), :])
    vb, ib = _top_rows(st_ref[pl.ds(base + nk, nk), :])
    cand, order, cidx = _pair_candidates(va, ia, vb, ib)
    tv, ti = _top_rows(cand, order, cidx)
    e = jnp.exp(tv - tv[0:1])
    ob = pl.multiple_of(h * PEER_TOPK, PEER_TOPK)
    idx_ref[pl.ds(ob, PEER_TOPK), :] = ti * PACK_ROWS
    g_ref[pl.ds(ob, PEER_TOPK), :] = e / jnp.sum(e, axis=0, keepdims=True)


def _topk_kernel(st_ref, idx_ref, g_ref):
    def head(h, c):
        _topk_head(st_ref, idx_ref, g_ref, h)
        return c

    lax.fori_loop(0, PEER_HEADS, head, 0)


def _topk(st, *, tm, first, count):
    ns, _ = st.shape
    nsel = PEER_HEADS * PEER_TOPK
    return pl.pallas_call(
        _topk_kernel,
        grid=(count,),
        in_specs=[pl.BlockSpec((ns, tm), lambda i: (0, i + first))],
        out_specs=[pl.BlockSpec((nsel, tm), lambda i: (0, i)), pl.BlockSpec((nsel, tm), lambda i: (0, i))],
        out_shape=[jax.ShapeDtypeStruct((nsel, count * tm), jnp.int32),
                   jax.ShapeDtypeStruct((nsel, count * tm), F32)],
        compiler_params=pltpu.CompilerParams(dimension_semantics=("parallel",), vmem_limit_bytes=VMEM_LIMIT),
        name="topk",
    )(st)


def _gather_scratch(nsel):
    return ([pltpu.SMEM((IDX_SLOTS, IDX_GROUP, nsel), jnp.int32), pltpu.SemaphoreType.DMA((IDX_SLOTS,))]
            + [pltpu.VMEM((PACK_ROWS * TILE_STRIDE, LANES), jnp.int32) for _ in range(GATHER_TILES)])


def _tile_rows(tile_ref, nsel):
    x = jnp.concatenate([tile_ref[q * TILE_STRIDE:q * TILE_STRIDE + nsel, :] for q in range(PACK_ROWS)], axis=1)
    return pltpu.bitcast(x, BF16)


def _pipelined_tokens(tm, nsel, idx_hbm, idx_s, sems, tab_ref, tiles, compute, side_work=None):
    n = len(tiles)
    span = IDX_SLOTS * IDX_GROUP
    groups = tm // IDX_GROUP
    assert tm % span == 0 and span % n == 0 and GATHER_LEAD < min(n, IDX_GROUP)
    g0 = pl.program_id(0) * groups

    def idx_copy(group, slot):
        g = g0 + jnp.minimum(group, groups - 1)
        return pltpu.make_async_copy(idx_hbm.at[g], idx_s.at[slot], sems.at[slot])

    def gather(pos, tile_ref):
        slot, row = (pos // IDX_GROUP) % IDX_SLOTS, pos % IDX_GROUP
        for mi in range(nsel):
            i = pl.multiple_of(idx_s[slot, row, mi], PACK_ROWS)
            tile_ref[pl.ds(mi, PACK_ROWS, stride=TILE_STRIDE), :] = tab_ref[pl.ds(i, PACK_ROWS), :]

    for s in range(IDX_SLOTS):
        idx_copy(s, s).start()
    idx_copy(0, 0).wait()
    for j in range(GATHER_LEAD):
        gather(j, tiles[j % n])

    def body(b, c):
        if side_work is not None:
            side_work(b, tm // span)
        for j in range(span):
            pos = j + GATHER_LEAD
            slot = (pos // IDX_GROUP) % IDX_SLOTS
            group = b * IDX_SLOTS + pos // IDX_GROUP
            if pos % IDX_GROUP == 0:
                idx_copy(group, slot).wait()
            gather(pos, tiles[pos % n])
            if pos % IDX_GROUP == IDX_GROUP - 1:
                idx_copy(group + IDX_SLOTS, slot).start()
            compute(b * span + j, tiles[j % n])
        return c

    lax.fori_loop(0, tm // span, body, 0)
    for s in range(1, IDX_SLOTS):
        idx_copy(0, s).wait()


def _two_rows(lo, hi):
    row = lax.broadcasted_iota(jnp.int32, (8, lo.shape[1]), 0)
    return jnp.where(row < 4, jnp.broadcast_to(lo, (8, lo.shape[1])), jnp.broadcast_to(hi, (8, hi.shape[1]))).astype(BF16)


def _peer_u_kernel(idx_hbm, h_ref, g_ref, tab_ref, w_ref, z_ref, idx_s, sems, *tiles, tm, nsel, topk_refs=()):
    half = h_ref.shape[1] // 2

    def compute(t, tile_ref):
        hrow = h_ref[pl.ds(t, 1), :]
        z8 = _dot_nt(_two_rows(hrow[:, :half], hrow[:, half:]), _tile_rows(tile_ref, nsel))
        lane = lax.broadcasted_iota(jnp.int32, (1, 2 * nsel), 1)
        z_ref[pl.ds(t, 1), :] = jnp.where(lane % 2 == 0, z8[0:1], z8[4:5])

    side_work = None
    if topk_refs:
        st_ref, idx_out_ref, g_out_ref = topk_refs

        def side_work(b, nb):
            assert PEER_HEADS % nb == 0
            for u in range(PEER_HEADS // nb):
                _topk_head(st_ref, idx_out_ref, g_out_ref, b * (PEER_HEADS // nb) + u)

    _pipelined_tokens(tm, nsel, idx_hbm, idx_s, sems, tab_ref, tiles, compute, side_work)
    r = lax.broadcasted_iota(jnp.int32, (2 * nsel, nsel), 0)
    col = lax.broadcasted_iota(jnp.int32, (2 * nsel, nsel), 1)
    pair = (r // 2 == col).astype(F32)
    act = jnp.dot(z_ref[...], pair, precision=lax.Precision.HIGHEST, preferred_element_type=F32)
    gelu = 0.5 * act * (1.0 + lax.erf(act * (1.0 / math.sqrt(2.0))))
    w_ref[...] = g_ref[...] * gelu


def _peer_u_fused_kernel(idx_hbm, h_ref, g_ref, tab_ref, st_ref, w_ref, idx_out_ref, g_out_ref, *scratch, tm, nsel):
    _peer_u_kernel(idx_hbm, h_ref, g_ref, tab_ref, w_ref, *scratch, tm=tm, nsel=nsel,
                   topk_refs=(st_ref, idx_out_ref, g_out_ref))


def _peer_u(idx, h2, g, tab, *, tm, first, st=None, st_first=0):
    d = h2.shape[1]
    nsel = idx.shape[2]
    count = idx.shape[0] * IDX_GROUP // tm
    part = lambda w: pl.BlockSpec((tm, w), lambda i: (i, 0))
    in_specs = [pl.BlockSpec(memory_space=pl.ANY), pl.BlockSpec((tm, d), lambda i: (i + first, 0)), part(nsel),
                _const_spec(tab.shape)]
    out_specs = [part(nsel)]
    out_shape = [jax.ShapeDtypeStruct((count * tm, nsel), F32)]
    args = [idx, h2, g, tab]
    body = functools.partial(_peer_u_kernel, tm=tm, nsel=nsel)
    if st is not None:
        cols = lambda dt: jax.ShapeDtypeStruct((nsel, count * tm), dt)
        in_specs.append(pl.BlockSpec((st.shape[0], tm), lambda i: (0, i + st_first)))
        out_specs += [pl.BlockSpec((nsel, tm), lambda i: (0, i)), pl.BlockSpec((nsel, tm), lambda i: (0, i))]
        out_shape += [cols(jnp.int32), cols(F32)]
        args.append(st)
        body = functools.partial(_peer_u_fused_kernel, tm=tm, nsel=nsel)
    return pl.pallas_call(
        body,
        grid=(count,),
        in_specs=in_specs,
        out_specs=out_specs,
        out_shape=out_shape,
        scratch_shapes=[pltpu.VMEM((tm, 2 * nsel), F32)] + _gather_scratch(nsel),
        compiler_params=pltpu.CompilerParams(dimension_semantics=("parallel",), vmem_limit_bytes=VMEM_LIMIT),
        name="peer_u",
    )(*args)


def _peer_v_kernel(idx_hbm, w_ref, x2_ref, gfin_ref, tab_ref, y_ref, we_ref, wo_ref, po_ref, idx_s, sems, *tiles,
                   tm, nsel):
    wb = w_ref[...].astype(BF16)
    r = lax.broadcasted_iota(jnp.int32, (nsel, 2 * nsel), 0)
    col = lax.broadcasted_iota(jnp.int32, (nsel, 2 * nsel), 1)
    we_ref[...] = _dot(wb, (col == 2 * r).astype(BF16))
    wo_ref[...] = _dot(wb, (col == 2 * r + 1).astype(BF16))

    def compute(t, tile_ref):
        o8 = _dot(_two_rows(we_ref[pl.ds(t, 1), :], wo_ref[pl.ds(t, 1), :]), _tile_rows(tile_ref, nsel))
        po_ref[pl.ds(t, 1), :] = jnp.concatenate([o8[0:1], o8[4:5]], axis=1)

    _pipelined_tokens(tm, nsel, idx_hbm, idx_s, sems, tab_ref, tiles, compute)
    y_ref[...] = _rms(x2_ref[...] + po_ref[...], gfin_ref[...])


def _peer_v(idx, w, x2, gfin, tab, *, tm, first):
    d = x2.shape[1]
    nsel = idx.shape[2]
    count = idx.shape[0] * IDX_GROUP // tm
    tok = lambda wd: pl.BlockSpec((tm, wd), lambda i: (i, 0))
    return pl.pallas_call(
        functools.partial(_peer_v_kernel, tm=tm, nsel=nsel),
        grid=(count,),
        in_specs=[pl.BlockSpec(memory_space=pl.ANY), tok(nsel), pl.BlockSpec((tm, d), lambda i: (i + first, 0)),
                  _const_spec(gfin.shape), _const_spec(tab.shape)],
        out_specs=tok(d),
        out_shape=jax.ShapeDtypeStruct((count * tm, d), F32),
        scratch_shapes=[pltpu.VMEM((tm, 2 * nsel), F32), pltpu.VMEM((tm, 2 * nsel), F32), pltpu.VMEM((tm, d), F32)]
        + _gather_scratch(nsel),
        compiler_params=pltpu.CompilerParams(dimension_semantics=("parallel",), vmem_limit_bytes=VMEM_LIMIT),
        name="peer_v",
    )(idx, w, x2, gfin, tab)


def _t5_bucket(rel):
    nb = REL_BUCKETS // 2
    max_exact = nb // 2
    ret = jnp.where(rel > 0, nb, 0)
    n = jnp.abs(rel)
    large = max_exact + (jnp.log(jnp.maximum(n, 1).astype(F32) / max_exact)
                         / math.log(REL_MAX_DIST / max_exact) * (nb - max_exact)).astype(jnp.int32)
    large = jnp.minimum(large, nb - 1)
    return ret + jnp.where(n < max_exact, n, large)


def _bias_tiles(rel_bias, q_pos, k_pos, k_valid):
    bucket = _t5_bucket(k_pos[None, :] - q_pos[:, None])
    table = rel_bias.astype(F32) * LOG2E
    bias = jnp.zeros((rel_bias.shape[1],) + bucket.shape, F32)
    for bkt in range(REL_BUCKETS):
        bias = jnp.where((bucket == bkt)[None], table[bkt][:, None, None], bias)
    visible = ((k_pos // CHUNK)[None, :] <= (q_pos // CHUNK)[:, None]) & k_valid[None, :]
    return jnp.where(visible[None], bias, NEG_INF)


def _rope_tables(pos):
    half = MLA_ROPE // 2
    inv = ROPE_THETA ** (-jnp.arange(half, dtype=F32) / half)
    ang = pos.astype(F32)[:, None] * inv
    cos, sin = jnp.cos(ang), jnp.sin(ang)
    pad = jnp.zeros((pos.shape[0], LANES - MLA_ROPE), F32)
    return jnp.concatenate([cos, cos, pad], axis=1), jnp.concatenate([-sin, sin, pad], axis=1)


def _pack_table(tab):
    n, d = tab.shape
    bits = lax.bitcast_convert_type(tab.astype(BF16), jnp.uint16).astype(jnp.uint32)
    words = bits[:, :d // 2] | (bits[:, d // 2:] << 16)
    return lax.bitcast_convert_type(words, jnp.int32).reshape(n * PACK_ROWS, LANES)


def _layer_weights(l, norm_mix, w_in, diff_lambda, diff_subln, mla_q_norm, mla_w_uq, mla_kv_norm, mla_w_uk, mla_w_uv,
                   w_branch_a, w_branch_b, w_out, norm_ffn, peer_w_q, peer_keys, peer_u, peer_v):
    d = w_in.shape[1]
    da = DA_HEADS * 2 * DA_HEAD_DIM
    q_lora = mla_w_uq.shape[1]
    kv_lora = mla_w_uk.shape[1]
    nh = mla_w_uq.shape[2]
    win = w_in[l]
    c0 = 3 * da
    cuts = [c0, c0 + q_lora, c0 + q_lora + kv_lora, c0 + q_lora + kv_lora + MLA_ROPE]
    qscale = jnp.concatenate([jnp.full((da,), DA_HEAD_DIM ** -0.5 * LOG2E, F32), jnp.ones((2 * da,), F32)])
    w = {}
    w["gmix"] = norm_mix[l][None]
    w["wqkv"] = (win[:, :c0] * qscale).astype(BF16)
    w["wcq"] = win[:, cuts[0]:cuts[1]].astype(BF16)
    w["wkv"] = jnp.concatenate([win[:, cuts[1]:cuts[3]], jnp.zeros((d, LANES - MLA_ROPE), F32)], axis=1).astype(BF16)
    w["wg"] = win[:, cuts[3]:].astype(BF16)
    w["gq"] = mla_q_norm[l][None]
    w["gkv"] = mla_kv_norm[l][None]
    uq = jnp.transpose(mla_w_uq[l], (1, 0, 2))
    uk = jnp.transpose(mla_w_uk[l], (1, 2, 0))
    wlat = _bmm(uq[:, :, :MLA_NOPE], uk)
    mla_scale = (MLA_NOPE + MLA_ROPE) ** -0.5 * LOG2E
    wqf = jnp.concatenate([wlat, uq[:, :, MLA_NOPE:], jnp.zeros((nh, q_lora, LANES - MLA_ROPE), F32)], axis=2) * mla_scale
    w["wqf"] = jnp.transpose(wqf, (1, 0, 2)).reshape(q_lora, nh * 2 * LANES).astype(BF16)
    w["lam"] = (jnp.exp(jnp.sum(diff_lambda[l][0] * diff_lambda[l][1]))
                - jnp.exp(jnp.sum(diff_lambda[l][2] * diff_lambda[l][3]))).astype(F32)
    w["subln"] = diff_subln[l][None]
    w["wa"] = w_branch_a[l].astype(BF16)
    uv = jnp.transpose(mla_w_uv[l], (1, 0, 2))
    wbb = w_branch_b[l].reshape(nh, uv.shape[2], d)
    w["wb"] = _bmm(uv, wbb).reshape(nh * kv_lora, d).astype(BF16)
    w["wo"] = w_out[l].astype(BF16)
    w["gffn"] = norm_ffn[l][None]
    ng = PEER_HEADS * 2
    dk = peer_keys.shape[-1]
    keys = peer_keys[l].reshape(ng, PEER_N_KEYS, dk)
    wq_t = jnp.transpose(peer_w_q[l]).reshape(ng, dk, d)
    w["wst"] = _bmm(keys, wq_t).reshape(ng * PEER_N_KEYS, d).astype(BF16)
    w["utab"] = _pack_table(peer_u[l])
    w["vtab"] = _pack_table(peer_v[l])
    return w


def _group_forward(x, pos, past, layer_idx, rel_bias, w, gfin, *, tm, tm_post, tq_diff, tq_mla, rs_diff, rs_mla,
                   tm_peer, peer_parts):
    b, s, d = x.shape
    t = b * s
    lambda_init = 0.8 - 0.6 * math.exp(-0.3 * layer_idx)
    cos_t, sin_t = _rope_tables(pos)
    xf = x.reshape(t, d)
    dq, dk, dv, dkb, dvb, qm, ckv, kpe, kf, gate = _in_proj(
        xf, cos_t, sin_t, w["gmix"], w["wqkv"], w["wcq"], w["wkv"], w["wg"], w["gq"], w["gkv"], w["wqf"], seq=s, tm=tm)
    lam = jnp.reshape(w["lam"] + lambda_init, (1,))
    nb = REL_BUCKETS // 2
    cfar = rel_bias[nb - 1].astype(F32) * LOG2E
    if past is None:
        assert s % tq_diff == 0 and s % tq_mla == 0 and tq_diff % CHUNK == 0 and tq_mla % CHUNK == 0
        assert tq_diff >= REL_MAX_DIST
        tq = tq_diff
        q_pos = jnp.arange(tq, 2 * tq, dtype=jnp.int32)
        k_pos = jnp.arange(0, 2 * tq, dtype=jnp.int32)
        both = _bias_tiles(rel_bias, q_pos, k_pos, jnp.ones((2 * tq,), bool))
        far = jnp.broadcast_to(cfar[:, None, None], (cfar.shape[0], tq, tq))
        bias = jnp.stack([far, both[:, :, :tq], both[:, :, tq:]], axis=1)
        oa = _diff_attn(dq.reshape(b, s, d), dkb.reshape(b, s, d), dvb.reshape(b, s, d), bias, lam, w["subln"],
                        tq=tq, tk=tq, rs=rs_diff, causal=True, out_scale=1.0 - lambda_init)
        tqm = tq_mla
        lp = jnp.arange(tqm, dtype=jnp.int32)
        diag = jnp.where((lp // CHUNK)[None, :] <= (lp // CHUNK)[:, None], 0.0, NEG_INF).astype(F32)
        mask = jnp.stack([jnp.zeros_like(diag), diag])
        ol = _mla_attn(qm.reshape(b, s, -1), kf.reshape(b, s, -1), mask, tq=tqm, tk=tqm, rs=rs_mla,
                       nh=MLA_HEADS // 2, causal=True)
    else:
        pk, pv, pckv, pkpe = past
        p = pk.shape[1]
        sk = -(-(p + s) // LANES) * LANES
        npad = sk - p - s
        k_pos = jnp.concatenate([jnp.arange(p, dtype=jnp.int32), pos, jnp.zeros((npad,), jnp.int32)])
        k_valid = jnp.arange(sk) < p + s
        zpad = jnp.zeros((b, npad, d), BF16)
        k_all = jnp.concatenate([pk.reshape(b, p, d).astype(BF16), dkb.reshape(b, s, d), zpad], axis=1)
        v_all = jnp.concatenate([pv.reshape(b, p, d).astype(BF16), dvb.reshape(b, s, d), zpad], axis=1)
        bias = _bias_tiles(rel_bias, pos, k_pos, k_valid)[:, None]
        oa = _diff_attn(dq.reshape(b, s, d), k_all, v_all, bias, lam, w["subln"],
                        tq=s, tk=sk, rs=rs_diff, causal=False, out_scale=1.0 - lambda_init)
        pad = jnp.zeros((b, p, LANES - MLA_ROPE), BF16)
        kf_past = jnp.concatenate([pckv.astype(BF16), pkpe.astype(BF16), pad], axis=2)
        kf_all = jnp.concatenate([kf_past, kf.reshape(b, s, -1), jnp.zeros((b, npad, 2 * LANES), BF16)], axis=1)
        padded = jnp.broadcast_to(jnp.where(k_valid, 0.0, NEG_INF).astype(F32)[None], (s, sk))
        mask = jnp.stack([jnp.zeros_like(padded), padded])
        ol = _mla_attn(qm.reshape(b, s, -1), kf_all, mask, tq=s, tk=sk, rs=rs_mla, nh=MLA_HEADS, causal=False)
    x2, h2, st = _post(xf, oa.reshape(t, d), ol.reshape(t, -1), gate, w["wa"], w["wb"], w["wo"], w["gffn"], w["wst"],
                       tm=tm_post)
    tiles = t // tm_peer
    assert tiles % peer_parts == 0
    per = tiles // peer_parts
    idx_t, g_t = _topk(st, tm=tm_peer, first=0, count=per)
    ys = []
    for part in range(peer_parts):
        first = part * per
        idx = jnp.transpose(idx_t).reshape(per * tm_peer // IDX_GROUP, IDX_GROUP, -1)
        g = jnp.transpose(g_t)
        if part + 1 < peer_parts:
            wts, idx_t, g_t = _peer_u(idx, h2, g, w["utab"], tm=tm_peer, first=first, st=st, st_first=first + per)
        else:
            wts, = _peer_u(idx, h2, g, w["utab"], tm=tm_peer, first=first)
        ys.append(_peer_v(idx, wts, x2, gfin, w["vtab"], tm=tm_peer, first=first))
    y = ys[0] if peer_parts == 1 else jnp.concatenate(ys, axis=0)
    return y.reshape(b, s, d), (dk, dv, ckv, kpe)


def kernel(x_prompt, x_sample, cache_diff_k, cache_diff_v, cache_mla_ckv, cache_mla_kpe, rel_bias, norm_mix, w_in,
           diff_lambda, diff_subln, mla_q_norm, mla_w_uq, mla_kv_norm, mla_w_uk, mla_w_uv, w_branch_a, w_branch_b,
           w_out, norm_ffn, peer_w_q, peer_keys, peer_u, peer_v, norm_final):
    depth = w_in.shape[0]
    assert depth == 1, "the final RMSNorm is fused into the last layer's PEER kernel"
    bp, sp, d = x_prompt.shape
    bs, ss, _ = x_sample.shape
    past_len = cache_diff_k.shape[2]
    pos_p = jnp.arange(sp, dtype=jnp.int32)
    pos_s = past_len + jnp.arange(ss, dtype=jnp.int32)
    gfin = norm_final[None]
    l = 0
    w = _layer_weights(l, norm_mix, w_in, diff_lambda, diff_subln, mla_q_norm, mla_w_uq, mla_kv_norm, mla_w_uk,
                       mla_w_uv, w_branch_a, w_branch_b, w_out, norm_ffn, peer_w_q, peer_keys, peer_u, peer_v)
    yp, (kp, vp, cp, ep) = _group_forward(x_prompt, pos_p, None, l, rel_bias, w, gfin,
                                          tm=min(256, sp), tm_post=256, tq_diff=min(512, sp), tq_mla=min(512, sp),
                                          rs_diff=128, rs_mla=128, tm_peer=128,
                                          peer_parts=4 if (bp * sp) % (4 * 128) == 0 else 1)
    past = (cache_diff_k[l], cache_diff_v[l], cache_mla_ckv[l], cache_mla_kpe[l])
    ys, (ks, vs, cs, es) = _group_forward(x_sample, pos_s, past, l, rel_bias, w, gfin,
                                          tm=ss, tm_post=128, tq_diff=ss, tq_mla=ss, rs_diff=ss, rs_mla=ss,
                                          tm_peer=128, peer_parts=1)
    nh, hd = DA_HEADS, DA_HEAD_DIM
    return (yp, ys,
            kp.reshape(1, bp, sp, nh, 2, hd), vp.reshape(1, bp, sp, nh, 2 * hd),
            cp.reshape(1, bp, sp, -1), ep.reshape(1, bp, sp, -1),
            ks.reshape(1, bs, ss, nh, 2, hd), vs.reshape(1, bs, ss, nh, 2 * hd),
            cs.reshape(1, bs, ss, -1), es.reshape(1, bs, ss, -1))
```

```python
import functools
import math

import jax
import jax.numpy as jnp
from jax import lax
from jax.experimental import pallas as pl
from jax.experimental.pallas import tpu as pltpu

F32 = jnp.float32
BF16 = jnp.bfloat16

CHUNK = 64
DA_HEADS = 8
DA_HEAD_DIM = 64
MLA_HEADS = 16
MLA_NOPE = 64
MLA_ROPE = 32
ROPE_THETA = 10000.0
REL_BUCKETS = 32
REL_MAX_DIST = 128
PEER_HEADS = 8
PEER_N_KEYS = 128
PEER_TOPK = 16
NORM_EPS = 1e-6
NEG_INF = -1e30
LOG2E = math.log2(math.e)

LANES = 128
PACK_ROWS = 4
TILE_STRIDE = 136
VMEM_LIMIT = 56 * 1024 * 1024
GROUPS_PER_ITER = 4
GATHER_TILES = 4
GATHER_LEAD = 2
IDX_GROUP = 8
IDX_SLOTS = 4


def _const_spec(shape):
    nd = len(shape)
    return pl.BlockSpec(shape, lambda *_: (0,) * nd, pipeline_mode=pl.Buffered(1))


def _rms(x, g):
    return x * lax.rsqrt(jnp.mean(x * x, axis=-1, keepdims=True) + NORM_EPS) * g


def _dot(a, b):
    return jnp.dot(a, b, preferred_element_type=F32)


def _dot_nt(a, b):
    return lax.dot_general(a, b, (((1,), (1,)), ((), ())), preferred_element_type=F32)


def _bmm_kernel(a_ref, b_ref, o_ref):
    o_ref[...] = jnp.dot(a_ref[...], b_ref[...], precision=lax.Precision.HIGHEST, preferred_element_type=F32)


def _bmm(a, b):
    n, m, k = a.shape
    _, _, p = b.shape
    return pl.pallas_call(
        _bmm_kernel,
        grid=(n,),
        in_specs=[pl.BlockSpec((None, m, k), lambda i: (i, 0, 0)), pl.BlockSpec((None, k, p), lambda i: (i, 0, 0))],
        out_specs=pl.BlockSpec((None, m, p), lambda i: (i, 0, 0)),
        out_shape=jax.ShapeDtypeStruct((n, m, p), F32),
        name="fold_bmm",
    )(a, b)


def _in_proj_kernel(x_ref, cos_ref, sin_ref, gmix_ref, wqkv_ref, wcq_ref, wkv_ref, wg_ref, gq_ref, gkv_ref, wqf_ref,
                    dq_ref, dk_ref, dv_ref, dkb_ref, dvb_ref, qm_ref, ckv_ref, kpe_ref, kf_ref, gate_ref, *, d, nh):
    hb = _rms(x_ref[...], gmix_ref[...]).astype(BF16)
    z = _dot(hb, wqkv_ref[...])
    dq_ref[...] = z[:, :d].astype(BF16)
    dk = z[:, d:2 * d]
    dk_ref[...] = dk
    dkb_ref[...] = dk.astype(BF16)
    dv = z[:, 2 * d:]
    dv_ref[...] = dv
    dvb_ref[...] = dv.astype(BF16)

    cosb = cos_ref[...]
    sinb = sin_ref[...]
    lane = lax.broadcasted_iota(jnp.int32, cosb.shape, 1)
    half = MLA_ROPE // 2

    def rope(blk):
        swapped = jnp.where(lane < half, pltpu.roll(blk, LANES - half, 1), pltpu.roll(blk, half, 1))
        return blk * cosb + swapped * sinb

    zkv = _dot(hb, wkv_ref[...])
    ckv = _rms(zkv[:, :LANES], gkv_ref[...])
    ckv_ref[...] = ckv
    kr = rope(zkv[:, LANES:])
    kpe_ref[...] = kr[:, :MLA_ROPE]
    kf_ref[...] = jnp.concatenate([ckv.astype(BF16), kr.astype(BF16)], axis=1)

    cq = _rms(_dot(hb, wcq_ref[...]), gq_ref[...]).astype(BF16)
    qraw = _dot(cq, wqf_ref[...])
    for h in range(nh):
        c0 = 2 * LANES * h
        qm_ref[:, c0:c0 + LANES] = qraw[:, c0:c0 + LANES].astype(BF16)
        qm_ref[:, c0 + LANES:c0 + 2 * LANES] = rope(qraw[:, c0 + LANES:c0 + 2 * LANES]).astype(BF16)

    zg = _dot(hb, wg_ref[...])
    gate_ref[...] = (1.0 / (1.0 + jnp.exp(-zg))).astype(BF16)


def _in_proj(x, cos_t, sin_t, gmix, wqkv, wcq, wkv, wg, gq, gkv, wqf, *, seq, tm):
    t, d = x.shape
    nh = wqf.shape[1] // (2 * LANES)
    nblk = seq // tm
    tok = lambda w: pl.BlockSpec((tm, w), lambda i: (i, 0))
    pos = pl.BlockSpec((tm, LANES), lambda i: (i % nblk, 0))
    outs = [
        (d, BF16), (d, F32), (d, F32), (d, BF16), (d, BF16), (wqf.shape[1], BF16), (LANES, F32), (MLA_ROPE, F32),
        (2 * LANES, BF16), (2 * d, BF16),
    ]
    return pl.pallas_call(
        functools.partial(_in_proj_kernel, d=d, nh=nh),
        grid=(t // tm,),
        in_specs=[tok(d), pos, pos, _const_spec(gmix.shape), _const_spec(wqkv.shape), _const_spec(wcq.shape),
                  _const_spec(wkv.shape), _const_spec(wg.shape), _const_spec(gq.shape), _const_spec(gkv.shape),
                  _const_spec(wqf.shape)],
        out_specs=[tok(w) for w, _ in outs],
        out_shape=[jax.ShapeDtypeStruct((t, w), dt) for w, dt in outs],
        compiler_params=pltpu.CompilerParams(dimension_semantics=("parallel",), vmem_limit_bytes=VMEM_LIMIT),
        name="in_proj",
    )(x, cos_t, sin_t, gmix, wqkv, wcq, wkv, wg, gq, gkv, wqf)


def _attn_pipeline(q_ref, k_at, v_at, n_steps, bias_rows, state, *, rs):
    s_ref, p_ref, m_ref, l_ref, a_ref, acc_ref = state
    rows, tk = p_ref.shape
    nrep = tk // LANES
    ngroups = min(GROUPS_PER_ITER, rows // rs)
    chunk = rs * ngroups
    assert rows % chunk == 0
    m_ref[...] = jnp.full(m_ref.shape, -jnp.inf, F32)
    for ref in (l_ref, a_ref, acc_ref, p_ref):
        ref[...] = jnp.zeros(ref.shape, ref.dtype)
    s_ref[0] = _dot_nt(q_ref[...], k_at(0))

    def pv(r0, j):
        rr = pl.ds(r0, chunk)
        acc_ref[rr, :] = a_ref[rr, :] * acc_ref[rr, :] + _dot(p_ref[rr, :], v_at(j))

    def step(j, c):
        cur = lax.rem(j, 2)
        j_next = jnp.minimum(j + 1, n_steps - 1)
        j_prev = jnp.maximum(j - 1, 0)

        def chunk_iter(r, c2):
            r0 = pl.multiple_of(r * chunk, chunk)
            first = r == 0
            pv(pl.multiple_of(jnp.where(first, rows - chunk, r0 - chunk), chunk), jnp.where(first, j_prev, j))
            for u in range(ngroups):
                g0 = pl.multiple_of(r0 + u * rs, rs)
                s = bias_rows(s_ref[cur, pl.ds(g0, rs), :], j, g0)
                m_old = m_ref[pl.ds(g0, rs), :]
                m_new = jnp.maximum(m_old, jnp.max(s, axis=-1, keepdims=True))
                alpha = jnp.exp2(m_old - m_new)
                p = jnp.exp2(s - jnp.concatenate([m_new] * nrep, axis=1))
                psum = p[:, :LANES]
                for jj in range(1, nrep):
                    psum = psum + p[:, jj * LANES:(jj + 1) * LANES]
                l_ref[pl.ds(g0, rs), :] = alpha * l_ref[pl.ds(g0, rs), :] + psum
                m_ref[pl.ds(g0, rs), :] = m_new
                a_ref[pl.ds(g0, rs), :] = alpha
                p_ref[pl.ds(g0, rs), :] = p.astype(BF16)
            s_ref[1 - cur, pl.ds(r0, chunk), :] = _dot_nt(q_ref[pl.ds(r0, chunk), :], k_at(j_next))
            return c2

        lax.fori_loop(0, rows // chunk, chunk_iter, 0)
        return c

    lax.fori_loop(0, n_steps, step, 0)
    pv(rows - chunk, n_steps - 1)
    return acc_ref[...] / jnp.sum(l_ref[...], axis=-1, keepdims=True)


def _attn_state_shapes(rows, tk):
    return [pltpu.VMEM((2, rows, tk), F32), pltpu.VMEM((rows, tk), BF16), pltpu.VMEM((rows, LANES), F32),
            pltpu.VMEM((rows, LANES), F32), pltpu.VMEM((rows, LANES), F32), pltpu.VMEM((rows, LANES), F32)]


def _diff_attn_kernel(lam_ref, q_ref, k_ref, v_ref, bias_ref, g_ref, o_ref, q2_ref, *state,
                      tq, tk, rs, causal, out_scale):
    i = pl.program_id(2)
    q = q_ref[...]
    lane = lax.broadcasted_iota(jnp.int32, q.shape, 1)
    zero = jnp.zeros_like(q)
    q2_ref[:tq] = jnp.where(lane < DA_HEAD_DIM, q, zero)
    q2_ref[tq:] = jnp.where(lane >= DA_HEAD_DIM, q, zero)
    n_near = bias_ref.shape[0]

    def bias_rows(s, j, g0):
        n = jnp.clip(j - (i + 1 - n_near), 0, n_near - 1) if causal else 0
        b0 = pl.multiple_of(g0 - jnp.where(g0 >= tq, tq, 0), rs)
        return s + bias_ref[n, pl.ds(b0, rs), :]

    def k_at(j):
        return k_ref[pl.ds(pl.multiple_of(j * tk, tk), tk), :]

    def v_at(j):
        return v_ref[pl.ds(pl.multiple_of(j * tk, tk), tk), :]

    o = _attn_pipeline(q2_ref, k_at, v_at, i + 1 if causal else 1, bias_rows, state, rs=rs)
    od = o[:tq] - lam_ref[0] * o[tq:]
    o_ref[...] = (_rms(od, g_ref[...]) * out_scale).astype(BF16)


def _diff_attn(q, k, v, bias, lam, g, *, tq, tk, rs, causal, out_scale):
    b, sq, d = q.shape
    sk = k.shape[1]
    nh = d // LANES
    n_near = bias.shape[1]
    assert tq % rs == 0 and tk % LANES == 0 and sk % tk == 0
    smem = pl.BlockSpec(memory_space=pltpu.SMEM)
    return pl.pallas_call(
        functools.partial(_diff_attn_kernel, tq=tq, tk=tk, rs=rs, causal=causal, out_scale=out_scale),
        grid=(b, nh, sq // tq),
        in_specs=[smem,
                  pl.BlockSpec((None, tq, LANES), lambda bb, hh, ii: (bb, ii, hh)),
                  pl.BlockSpec((None, sk, LANES), lambda bb, hh, ii: (bb, 0, hh)),
                  pl.BlockSpec((None, sk, LANES), lambda bb, hh, ii: (bb, 0, hh)),
                  pl.BlockSpec((None, n_near, tq, bias.shape[3]), lambda bb, hh, ii: (hh, 0, 0, 0)),
                  pl.BlockSpec((1, LANES), lambda bb, hh, ii: (0, 0))],
        out_specs=pl.BlockSpec((None, tq, LANES), lambda bb, hh, ii: (bb, ii, hh)),
        out_shape=jax.ShapeDtypeStruct((b, sq, d), BF16),
        scratch_shapes=[pltpu.VMEM((2 * tq, LANES), BF16)] + _attn_state_shapes(2 * tq, tk),
        compiler_params=pltpu.CompilerParams(dimension_semantics=("parallel", "parallel", "parallel"),
                                             vmem_limit_bytes=VMEM_LIMIT),
        name="diff_attn",
    )(lam, q, k, v, bias, g)


def _mla_attn_kernel(q_ref, kf_ref, mask_ref, o_ref, qs_ref, *state, tq, tk, rs, nh, causal):
    i = pl.program_id(2)
    for h in range(nh):
        qs_ref[h * tq:(h + 1) * tq, :] = q_ref[:, 2 * LANES * h:2 * LANES * (h + 1)]
    last = i if causal else 0

    def bias_rows(s, j, g0):
        n = jnp.where(j == last, 1, 0)
        return s + mask_ref[n, pl.ds(pl.multiple_of(lax.rem(g0, tq), rs), rs), :]

    def k_at(j):
        return kf_ref[pl.ds(pl.multiple_of(j * tk, tk), tk), :]

    def v_at(j):
        return kf_ref[pl.ds(pl.multiple_of(j * tk, tk), tk), :LANES]

    o = _attn_pipeline(qs_ref, k_at, v_at, last + 1, bias_rows, state, rs=rs)
    for h in range(nh):
        o_ref[:, LANES * h:LANES * (h + 1)] = o[h * tq:(h + 1) * tq].astype(BF16)


def _mla_attn(q, kf, mask, *, tq, tk, rs, nh, causal):
    b, sq, w = q.shape
    sk = kf.shape[1]
    ngrp = w // (2 * LANES * nh)
    assert tq % rs == 0 and tk % LANES == 0 and sk % tk == 0 and mask.shape == (2, tq, tk)
    return pl.pallas_call(
        functools.partial(_mla_attn_kernel, tq=tq, tk=tk, rs=rs, nh=nh, causal=causal),
        grid=(b, ngrp, sq // tq),
        in_specs=[pl.BlockSpec((None, tq, nh * 2 * LANES), lambda bb, gg, ii: (bb, ii, gg)),
                  pl.BlockSpec((None, sk, 2 * LANES), lambda bb, gg, ii: (bb, 0, 0)),
                  pl.BlockSpec(mask.shape, lambda bb, gg, ii: (0, 0, 0))],
        out_specs=pl.BlockSpec((None, tq, nh * LANES), lambda bb, gg, ii: (bb, ii, gg)),
        out_shape=jax.ShapeDtypeStruct((b, sq, ngrp * nh * LANES), BF16),
        scratch_shapes=[pltpu.VMEM((nh * tq, 2 * LANES), BF16)] + _attn_state_shapes(nh * tq, tk),
        compiler_params=pltpu.CompilerParams(dimension_semantics=("parallel", "parallel", "parallel"),
                                             vmem_limit_bytes=VMEM_LIMIT),
        name="mla_attn",
    )(q, kf, mask)


def _post_kernel(x_ref, oa_ref, ol_ref, gate_ref, wa_ref, wb_ref, wo_ref, gffn_ref, wst_ref, x2_ref, h2_ref, st_ref,
                 *, d):
    a = _dot(oa_ref[...], wa_ref[...])
    b = _dot(ol_ref[...], wb_ref[...])
    g = gate_ref[...].astype(F32)
    merged = g[:, :d] * a + g[:, d:] * b
    x2 = x_ref[...] + _dot(merged.astype(BF16), wo_ref[...])
    x2_ref[...] = x2
    h2 = _rms(x2, gffn_ref[...])
    h2_ref[...] = h2
    st_ref[...] = _dot_nt(wst_ref[...], h2.astype(BF16))


def _post(x, oa, ol, gate, wa, wb, wo, gffn, wst, *, tm):
    t, d = x.shape
    ns = wst.shape[0]
    tok = lambda w: pl.BlockSpec((tm, w), lambda i: (i, 0))
    return pl.pallas_call(
        functools.partial(_post_kernel, d=d),
        grid=(t // tm,),
        in_specs=[tok(d), tok(oa.shape[1]), tok(ol.shape[1]), tok(gate.shape[1]), _const_spec(wa.shape),
                  _const_spec(wb.shape), _const_spec(wo.shape), _const_spec(gffn.shape), _const_spec(wst.shape)],
        out_specs=[tok(d), tok(d), pl.BlockSpec((ns, tm), lambda i: (0, i))],
        out_shape=[jax.ShapeDtypeStruct((t, d), F32), jax.ShapeDtypeStruct((t, d), F32),
                   jax.ShapeDtypeStruct((ns, t), F32)],
        compiler_params=pltpu.CompilerParams(dimension_semantics=("parallel",), vmem_limit_bytes=VMEM_LIMIT),
        name="post",
    )(x, oa, ol, gate, wa, wb, wo, gffn, wst)


def _top_rows(s, order=None, payload=None):
    if order is None:
        order = lax.broadcasted_iota(jnp.int32, s.shape, 0)
    big = jnp.int32(2 ** 30)
    vals, picks = [], []
    for _ in range(PEER_TOPK):
        m = jnp.max(s, axis=0, keepdims=True)
        pos = jnp.min(jnp.where(s == m, order, big), axis=0, keepdims=True)
        sel = order == pos
        vals.append(m)
        picks.append(pos if payload is None else jnp.sum(jnp.where(sel, payload, 0), axis=0, keepdims=True))
        s = jnp.where(sel, -jnp.inf, s)
    return jnp.concatenate(vals, axis=0), jnp.concatenate(picks, axis=0)


def _pair_candidates(va, ia, vb, ib):
    k, nk = PEER_TOPK, PEER_N_KEYS
    split = 4
    unused = jnp.int32(2 ** 29)
    vals, orders, idxs = [], [], []
    for a in range(split):
        n = k // (a + 1)
        rows = -(-n // 8) * 8
        r = lax.broadcasted_iota(jnp.int32, (rows, va.shape[1]), 0)
        valid = r < n
        vals.append(jnp.where(valid, va[a:a + 1] + vb[:rows], -jnp.inf))
        orders.append(jnp.where(valid, a * k + r, unused))
        idxs.append(ia[a:a + 1] * nk + ib[:rows])
    for b in range(k // (split + 1)):
        n = k // (b + 1)
        rows = -(-n // 8) * 8
        r = lax.broadcasted_iota(jnp.int32, (rows, va.shape[1]), 0)
        valid = (r >= split) & (r < n)
        vals.append(jnp.where(valid, va[:rows] + vb[b:b + 1], -jnp.inf))
        orders.append(jnp.where(valid, r * k + b, unused))
        idxs.append(ia[:rows] * nk + ib[b:b + 1])
    return jnp.concatenate(vals, axis=0), jnp.concatenate(orders, axis=0), jnp.concatenate(idxs, axis=0)


def _topk_head(st_ref, idx_ref, g_ref, h):
    nk = PEER_N_KEYS
    base = pl.multiple_of(h * 2 * nk, 2 * nk)
    va, ia = _top_rows(st_ref[pl.ds(base, nk), :])
    vb, ib = _top_rows(st_ref[pl.ds(base + nk, nk), :])
    cand, order, cidx = _pair_candidates(va, ia, vb, ib)
    tv, ti = _top_rows(cand, order, cidx)
    e = jnp.exp(tv - tv[0:1])
    ob = pl.multiple_of(h * PEER_TOPK, PEER_TOPK)
    idx_ref[pl.ds(ob, PEER_TOPK), :] = ti * PACK_ROWS
    g_ref[pl.ds(ob, PEER_TOPK), :] = e / jnp.sum(e, axis=0, keepdims=True)


def _topk_kernel(st_ref, idx_ref, g_ref):
    def head(h, c):
        _topk_head(st_ref, idx_ref, g_ref, h)
        return c

    lax.fori_loop(0, PEER_HEADS, head, 0)


def _topk(st, *, tm, first, count):
    ns, _ = st.shape
    nsel = PEER_HEADS * PEER_TOPK
    return pl.pallas_call(
        _topk_kernel,
        grid=(count,),
        in_specs=[pl.BlockSpec((ns, tm), lambda i: (0, i + first))],
        out_specs=[pl.BlockSpec((nsel, tm), lambda i: (0, i)), pl.BlockSpec((nsel, tm), lambda i: (0, i))],
        out_shape=[jax.ShapeDtypeStruct((nsel, count * tm), jnp.int32),
                   jax.ShapeDtypeStruct((nsel, count * tm), F32)],
        compiler_params=pltpu.CompilerParams(dimension_semantics=("parallel",), vmem_limit_bytes=VMEM_LIMIT),
        name="topk",
    )(st)


def _gather_scratch(nsel):
    return ([pltpu.SMEM((IDX_SLOTS, IDX_GROUP, nsel), jnp.int32), pltpu.SemaphoreType.DMA((IDX_SLOTS,))]
            + [pltpu.VMEM((PACK_ROWS * TILE_STRIDE, LANES), jnp.int32) for _ in range(GATHER_TILES)])


def _tile_rows(tile_ref, nsel):
    x = jnp.concatenate([tile_ref[q * TILE_STRIDE:q * TILE_STRIDE + nsel, :] for q in range(PACK_ROWS)], axis=1)
    return pltpu.bitcast(x, BF16)


def _pipelined_tokens(tm, nsel, idx_hbm, idx_s, sems, tab_ref, tiles, compute, side_work=None):
    n = len(tiles)
    span = IDX_SLOTS * IDX_GROUP
    groups = tm // IDX_GROUP
    assert tm % span == 0 and span % n == 0 and GATHER_LEAD < min(n, IDX_GROUP)
    g0 = pl.program_id(0) * groups

    def idx_copy(group, slot):
        g = g0 + jnp.minimum(group, groups - 1)
        return pltpu.make_async_copy(idx_hbm.at[g], idx_s.at[slot], sems.at[slot])

    def gather(pos, tile_ref):
        slot, row = (pos // IDX_GROUP) % IDX_SLOTS, pos % IDX_GROUP
        for mi in range(nsel):
            i = pl.multiple_of(idx_s[slot, row, mi], PACK_ROWS)
            tile_ref[pl.ds(mi, PACK_ROWS, stride=TILE_STRIDE), :] = tab_ref[pl.ds(i, PACK_ROWS), :]

    for s in range(IDX_SLOTS):
        idx_copy(s, s).start()
    idx_copy(0, 0).wait()
    for j in range(GATHER_LEAD):
        gather(j, tiles[j % n])

    def body(b, c):
        if side_work is not None:
            side_work(b, tm // span)
        for j in range(span):
            pos = j + GATHER_LEAD
            slot = (pos // IDX_GROUP) % IDX_SLOTS
            group = b * IDX_SLOTS + pos // IDX_GROUP
            if pos % IDX_GROUP == 0:
                idx_copy(group, slot).wait()
            gather(pos, tiles[pos % n])
            if pos % IDX_GROUP == IDX_GROUP - 1:
                idx_copy(group + IDX_SLOTS, slot).start()
            compute(b * span + j, tiles[j % n])
        return c

    lax.fori_loop(0, tm // span, body, 0)
    for s in range(1, IDX_SLOTS):
        idx_copy(0, s).wait()


def _two_rows(lo, hi):
    row = lax.broadcasted_iota(jnp.int32, (8, lo.shape[1]), 0)
    return jnp.where(row < 4, jnp.broadcast_to(lo, (8, lo.shape[1])), jnp.broadcast_to(hi, (8, hi.shape[1]))).astype(BF16)


def _peer_u_kernel(idx_hbm, h_ref, g_ref, tab_ref, w_ref, z_ref, idx_s, sems, *tiles, tm, nsel, topk_refs=()):
    half = h_ref.shape[1] // 2

    def compute(t, tile_ref):
        hrow = h_ref[pl.ds(t, 1), :]
        z8 = _dot_nt(_two_rows(hrow[:, :half], hrow[:, half:]), _tile_rows(tile_ref, nsel))
        lane = lax.broadcasted_iota(jnp.int32, (1, 2 * nsel), 1)
        z_ref[pl.ds(t, 1), :] = jnp.where(lane % 2 == 0, z8[0:1], z8[4:5])

    side_work = None
    if topk_refs:
        st_ref, idx_out_ref, g_out_ref = topk_refs

        def side_work(b, nb):
            assert PEER_HEADS % nb == 0
            for u in range(PEER_HEADS // nb):
                _topk_head(st_ref, idx_out_ref, g_out_ref, b * (PEER_HEADS // nb) + u)

    _pipelined_tokens(tm, nsel, idx_hbm, idx_s, sems, tab_ref, tiles, compute, side_work)
    r = lax.broadcasted_iota(jnp.int32, (2 * nsel, nsel), 0)
    col = lax.broadcasted_iota(jnp.int32, (2 * nsel, nsel), 1)
    pair = (r // 2 == col).astype(F32)
    act = jnp.dot(z_ref[...], pair, precision=lax.Precision.HIGHEST, preferred_element_type=F32)
    gelu = 0.5 * act * (1.0 + lax.erf(act * (1.0 / math.sqrt(2.0))))
    w_ref[...] = g_ref[...] * gelu


def _peer_u_fused_kernel(idx_hbm, h_ref, g_ref, tab_ref, st_ref, w_ref, idx_out_ref, g_out_ref, *scratch, tm, nsel):
    _peer_u_kernel(idx_hbm, h_ref, g_ref, tab_ref, w_ref, *scratch, tm=tm, nsel=nsel,
                   topk_refs=(st_ref, idx_out_ref, g_out_ref))


def _peer_u(idx, h2, g, tab, *, tm, first, st=None, st_first=0):
    d = h2.shape[1]
    nsel = idx.shape[2]
    count = idx.shape[0] * IDX_GROUP // tm
    part = lambda w: pl.BlockSpec((tm, w), lambda i: (i, 0))
    in_specs = [pl.BlockSpec(memory_space=pl.ANY), pl.BlockSpec((tm, d), lambda i: (i + first, 0)), part(nsel),
                _const_spec(tab.shape)]
    out_specs = [part(nsel)]
    out_shape = [jax.ShapeDtypeStruct((count * tm, nsel), F32)]
    args = [idx, h2, g, tab]
    body = functools.partial(_peer_u_kernel, tm=tm, nsel=nsel)
    if st is not None:
        cols = lambda dt: jax.ShapeDtypeStruct((nsel, count * tm), dt)
        in_specs.append(pl.BlockSpec((st.shape[0], tm), lambda i: (0, i + st_first)))
        out_specs += [pl.BlockSpec((nsel, tm), lambda i: (0, i)), pl.BlockSpec((nsel, tm), lambda i: (0, i))]
        out_shape += [cols(jnp.int32), cols(F32)]
        args.append(st)
        body = functools.partial(_peer_u_fused_kernel, tm=tm, nsel=nsel)
    return pl.pallas_call(
        body,
        grid=(count,),
        in_specs=in_specs,
        out_specs=out_specs,
        out_shape=out_shape,
        scratch_shapes=[pltpu.VMEM((tm, 2 * nsel), F32)] + _gather_scratch(nsel),
        compiler_params=pltpu.CompilerParams(dimension_semantics=("parallel",), vmem_limit_bytes=VMEM_LIMIT),
        name="peer_u",
    )(*args)


def _peer_v_kernel(idx_hbm, w_ref, x2_ref, gfin_ref, tab_ref, y_ref, we_ref, wo_ref, po_ref, idx_s, sems, *tiles,
                   tm, nsel):
    wb = w_ref[...].astype(BF16)
    r = lax.broadcasted_iota(jnp.int32, (nsel, 2 * nsel), 0)
    col = lax.broadcasted_iota(jnp.int32, (nsel, 2 * nsel), 1)
    we_ref[...] = _dot(wb, (col == 2 * r).astype(BF16))
    wo_ref[...] = _dot(wb, (col == 2 * r + 1).astype(BF16))

    def compute(t, tile_ref):
        o8 = _dot(_two_rows(we_ref[pl.ds(t, 1), :], wo_ref[pl.ds(t, 1), :]), _tile_rows(tile_ref, nsel))
        po_ref[pl.ds(t, 1), :] = jnp.concatenate([o8[0:1], o8[4:5]], axis=1)

    _pipelined_tokens(tm, nsel, idx_hbm, idx_s, sems, tab_ref, tiles, compute)
    y_ref[...] = _rms(x2_ref[...] + po_ref[...], gfin_ref[...])


def _peer_v(idx, w, x2, gfin, tab, *, tm, first):
    d = x2.shape[1]
    nsel = idx.shape[2]
    count = idx.shape[0] * IDX_GROUP // tm
    tok = lambda wd: pl.BlockSpec((tm, wd), lambda i: (i, 0))
    return pl.pallas_call(
        functools.partial(_peer_v_kernel, tm=tm, nsel=nsel),
        grid=(count,),
        in_specs=[pl.BlockSpec(memory_space=pl.ANY), tok(nsel), pl.BlockSpec((tm, d), lambda i: (i + first, 0)),
                  _const_spec(gfin.shape), _const_spec(tab.shape)],
        out_specs=tok(d),
        out_shape=jax.ShapeDtypeStruct((count * tm, d), F32),
        scratch_shapes=[pltpu.VMEM((tm, 2 * nsel), F32), pltpu.VMEM((tm, 2 * nsel), F32), pltpu.VMEM((tm, d), F32)]
        + _gather_scratch(nsel),
        compiler_params=pltpu.CompilerParams(dimension_semantics=("parallel",), vmem_limit_bytes=VMEM_LIMIT),
        name="peer_v",
    )(idx, w, x2, gfin, tab)


def _t5_bucket(rel):
    nb = REL_BUCKETS // 2
    max_exact = nb // 2
    ret = jnp.where(rel > 0, nb, 0)
    n = jnp.abs(rel)
    large = max_exact + (jnp.log(jnp.maximum(n, 1).astype(F32) / max_exact)
                         / math.log(REL_MAX_DIST / max_exact) * (nb - max_exact)).astype(jnp.int32)
    large = jnp.minimum(large, nb - 1)
    return ret + jnp.where(n < max_exact, n, large)


def _bias_tiles(rel_bias, q_pos, k_pos, k_valid):
    bucket = _t5_bucket(k_pos[None, :] - q_pos[:, None])
    table = rel_bias.astype(F32) * LOG2E
    bias = jnp.zeros((rel_bias.shape[1],) + bucket.shape, F32)
    for bkt in range(REL_BUCKETS):
        bias = jnp.where((bucket == bkt)[None], table[bkt][:, None, None], bias)
    visible = ((k_pos // CHUNK)[None, :] <= (q_pos // CHUNK)[:, None]) & k_valid[None, :]
    return jnp.where(visible[None], bias, NEG_INF)


def _rope_tables(pos):
    half = MLA_ROPE // 2
    inv = ROPE_THETA ** (-jnp.arange(half, dtype=F32) / half)
    ang = pos.astype(F32)[:, None] * inv
    cos, sin = jnp.cos(ang), jnp.sin(ang)
    pad = jnp.zeros((pos.shape[0], LANES - MLA_ROPE), F32)
    return jnp.concatenate([cos, cos, pad], axis=1), jnp.concatenate([-sin, sin, pad], axis=1)


def _pack_table(tab):
    n, d = tab.shape
    bits = lax.bitcast_convert_type(tab.astype(BF16), jnp.uint16).astype(jnp.uint32)
    words = bits[:, :d // 2] | (bits[:, d // 2:] << 16)
    return lax.bitcast_convert_type(words, jnp.int32).reshape(n * PACK_ROWS, LANES)


def _layer_weights(l, norm_mix, w_in, diff_lambda, diff_subln, mla_q_norm, mla_w_uq, mla_kv_norm, mla_w_uk, mla_w_uv,
                   w_branch_a, w_branch_b, w_out, norm_ffn, peer_w_q, peer_keys, peer_u, peer_v):
    d = w_in.shape[1]
    da = DA_HEADS * 2 * DA_HEAD_DIM
    q_lora = mla_w_uq.shape[1]
    kv_lora = mla_w_uk.shape[1]
    nh = mla_w_uq.shape[2]
    win = w_in[l]
    c0 = 3 * da
    cuts = [c0, c0 + q_lora, c0 + q_lora + kv_lora, c0 + q_lora + kv_lora + MLA_ROPE]
    qscale = jnp.concatenate([jnp.full((da,), DA_HEAD_DIM ** -0.5 * LOG2E, F32), jnp.ones((2 * da,), F32)])
    w = {}
    w["gmix"] = norm_mix[l][None]
    w["wqkv"] = (win[:, :c0] * qscale).astype(BF16)
    w["wcq"] = win[:, cuts[0]:cuts[1]].astype(BF16)
    w["wkv"] = jnp.concatenate([win[:, cuts[1]:cuts[3]], jnp.zeros((d, LANES - MLA_ROPE), F32)], axis=1).astype(BF16)
    w["wg"] = win[:, cuts[3]:].astype(BF16)
    w["gq"] = mla_q_norm[l][None]
    w["gkv"] = mla_kv_norm[l][None]
    uq = jnp.transpose(mla_w_uq[l], (1, 0, 2))
    uk = jnp.transpose(mla_w_uk[l], (1, 2, 0))
    wlat = _bmm(uq[:, :, :MLA_NOPE], uk)
    mla_scale = (MLA_NOPE + MLA_ROPE) ** -0.5 * LOG2E
    wqf = jnp.concatenate([wlat, uq[:, :, MLA_NOPE:], jnp.zeros((nh, q_lora, LANES - MLA_ROPE), F32)], axis=2) * mla_scale
    w["wqf"] = jnp.transpose(wqf, (1, 0, 2)).reshape(q_lora, nh * 2 * LANES).astype(BF16)
    w["lam"] = (jnp.exp(jnp.sum(diff_lambda[l][0] * diff_lambda[l][1]))
                - jnp.exp(jnp.sum(diff_lambda[l][2] * diff_lambda[l][3]))).astype(F32)
    w["subln"] = diff_subln[l][None]
    w["wa"] = w_branch_a[l].astype(BF16)
    uv = jnp.transpose(mla_w_uv[l], (1, 0, 2))
    wbb = w_branch_b[l].reshape(nh, uv.shape[2], d)
    w["wb"] = _bmm(uv, wbb).reshape(nh * kv_lora, d).astype(BF16)
    w["wo"] = w_out[l].astype(BF16)
    w["gffn"] = norm_ffn[l][None]
    ng = PEER_HEADS * 2
    dk = peer_keys.shape[-1]
    keys = peer_keys[l].reshape(ng, PEER_N_KEYS, dk)
    wq_t = jnp.transpose(peer_w_q[l]).reshape(ng, dk, d)
    w["wst"] = _bmm(keys, wq_t).reshape(ng * PEER_N_KEYS, d).astype(BF16)
    w["utab"] = _pack_table(peer_u[l])
    w["vtab"] = _pack_table(peer_v[l])
    return w


def _group_forward(x, pos, past, layer_idx, rel_bias, w, gfin, *, tm, tm_post, tq_diff, tq_mla, rs_diff, rs_mla,
                   tm_peer, peer_parts):
    b, s, d = x.shape
    t = b * s
    lambda_init = 0.8 - 0.6 * math.exp(-0.3 * layer_idx)
    cos_t, sin_t = _rope_tables(pos)
    xf = x.reshape(t, d)
    dq, dk, dv, dkb, dvb, qm, ckv, kpe, kf, gate = _in_proj(
        xf, cos_t, sin_t, w["gmix"], w["wqkv"], w["wcq"], w["wkv"], w["wg"], w["gq"], w["gkv"], w["wqf"], seq=s, tm=tm)
    lam = jnp.reshape(w["lam"] + lambda_init, (1,))
    nb = REL_BUCKETS // 2
    cfar = rel_bias[nb - 1].astype(F32) * LOG2E
    if past is None:
        assert s % tq_diff == 0 and s % tq_mla == 0 and tq_diff % CHUNK == 0 and tq_mla % CHUNK == 0
        assert tq_diff >= REL_MAX_DIST
        tq = tq_diff
        q_pos = jnp.arange(tq, 2 * tq, dtype=jnp.int32)
        k_pos = jnp.arange(0, 2 * tq, dtype=jnp.int32)
        both = _bias_tiles(rel_bias, q_pos, k_pos, jnp.ones((2 * tq,), bool))
        far = jnp.broadcast_to(cfar[:, None, None], (cfar.shape[0], tq, tq))
        bias = jnp.stack([far, both[:, :, :tq], both[:, :, tq:]], axis=1)
        oa = _diff_attn(dq.reshape(b, s, d), dkb.reshape(b, s, d), dvb.reshape(b, s, d), bias, lam, w["subln"],
                        tq=tq, tk=tq, rs=rs_diff, causal=True, out_scale=1.0 - lambda_init)
        tqm = tq_mla
        lp = jnp.arange(tqm, dtype=jnp.int32)
        diag = jnp.where((lp // CHUNK)[None, :] <= (lp // CHUNK)[:, None], 0.0, NEG_INF).astype(F32)
        mask = jnp.stack([jnp.zeros_like(diag), diag])
        ol = _mla_attn(qm.reshape(b, s, -1), kf.reshape(b, s, -1), mask, tq=tqm, tk=tqm, rs=rs_mla,
                       nh=MLA_HEADS // 2, causal=True)
    else:
        pk, pv, pckv, pkpe = past
        p = pk.shape[1]
        sk = -(-(p + s) // LANES) * LANES
        npad = sk - p - s
        k_pos = jnp.concatenate([jnp.arange(p, dtype=jnp.int32), pos, jnp.zeros((npad,), jnp.int32)])
        k_valid = jnp.arange(sk) < p + s
        zpad = jnp.zeros((b, npad, d), BF16)
        k_all = jnp.concatenate([pk.reshape(b, p, d).astype(BF16), dkb.reshape(b, s, d), zpad], axis=1)
        v_all = jnp.concatenate([pv.reshape(b, p, d).astype(BF16), dvb.reshape(b, s, d), zpad], axis=1)
        bias = _bias_tiles(rel_bias, pos, k_pos, k_valid)[:, None]
        oa = _diff_attn(dq.reshape(b, s, d), k_all, v_all, bias, lam, w["subln"],
                        tq=s, tk=sk, rs=rs_diff, causal=False, out_scale=1.0 - lambda_init)
        pad = jnp.zeros((b, p, LANES - MLA_ROPE), BF16)
        kf_past = jnp.concatenate([pckv.astype(BF16), pkpe.astype(BF16), pad], axis=2)
        kf_all = jnp.concatenate([kf_past, kf.reshape(b, s, -1), jnp.zeros((b, npad, 2 * LANES), BF16)], axis=1)
        padded = jnp.broadcast_to(jnp.where(k_valid, 0.0, NEG_INF).astype(F32)[None], (s, sk))
        mask = jnp.stack([jnp.zeros_like(padded), padded])
        ol = _mla_attn(qm.reshape(b, s, -1), kf_all, mask, tq=s, tk=sk, rs=rs_mla, nh=MLA_HEADS, causal=False)
    x2, h2, st = _post(xf, oa.reshape(t, d), ol.reshape(t, -1), gate, w["wa"], w["wb"], w["wo"], w["gffn"], w["wst"],
                       tm=tm_post)
    tiles = t // tm_peer
    assert tiles % peer_parts == 0
    per = tiles // peer_parts
    idx_t, g_t = _topk(st, tm=tm_peer, first=0, count=per)
    ys = []
    for part in range(peer_parts):
        first = part * per
        idx = jnp.transpose(idx_t).reshape(per * tm_peer // IDX_GROUP, IDX_GROUP, -1)
        g = jnp.transpose(g_t)
        if part + 1 < peer_parts:
            wts, idx_t, g_t = _peer_u(idx, h2, g, w["utab"], tm=tm_peer, first=first, st=st, st_first=first + per)
        else:
            wts, = _peer_u(idx, h2, g, w["utab"], tm=tm_peer, first=first)
        ys.append(_peer_v(idx, wts, x2, gfin, w["vtab"], tm=tm_peer, first=first))
    y = ys[0] if peer_parts == 1 else jnp.concatenate(ys, axis=0)
    return y.reshape(b, s, d), (dk, dv, ckv, kpe)


def kernel(x_prompt, x_sample, cache_diff_k, cache_diff_v, cache_mla_ckv, cache_mla_kpe, rel_bias, norm_mix, w_in,
           diff_lambda, diff_subln, mla_q_norm, mla_w_uq, mla_kv_norm, mla_w_uk, mla_w_uv, w_branch_a, w_branch_b,
           w_out, norm_ffn, peer_w_q, peer_keys, peer_u, peer_v, norm_final):
    depth = w_in.shape[0]
    assert depth == 1, "the final RMSNorm is fused into the last layer's PEER kernel"
    bp, sp, d = x_prompt.shape
    bs, ss, _ = x_sample.shape
    past_len = cache_diff_k.shape[2]
    pos_p = jnp.arange(sp, dtype=jnp.int32)
    pos_s = past_len + jnp.arange(ss, dtype=jnp.int32)
    gfin = norm_final[None]
    l = 0
    w = _layer_weights(l, norm_mix, w_in, diff_lambda, diff_subln, mla_q_norm, mla_w_uq, mla_kv_norm, mla_w_uk,
                       mla_w_uv, w_branch_a, w_branch_b, w_out, norm_ffn, peer_w_q, peer_keys, peer_u, peer_v)
    yp, (kp, vp, cp, ep) = _group_forward(x_prompt, pos_p, None, l, rel_bias, w, gfin,
                                          tm=min(256, sp), tm_post=256, tq_diff=min(512, sp), tq_mla=min(512, sp),
                                          rs_diff=128, rs_mla=128, tm_peer=128,
                                          peer_parts=4 if (bp * sp) % (4 * 128) == 0 else 1)
    past = (cache_diff_k[l], cache_diff_v[l], cache_mla_ckv[l], cache_mla_kpe[l])
    ys, (ks, vs, cs, es) = _group_forward(x_sample, pos_s, past, l, rel_bias, w, gfin,
                                          tm=ss, tm_post=128, tq_diff=ss, tq_mla=ss, rs_diff=ss, rs_mla=ss,
                                          tm_peer=128, peer_parts=1)
    nh, hd = DA_HEADS, DA_HEAD_DIM
    return (yp, ys,
            kp.reshape(1, bp, sp, nh, 2, hd), vp.reshape(1, bp, sp, nh, 2 * hd),
            cp.reshape(1, bp, sp, -1), ep.reshape(1, bp, sp, -1),
            ks.reshape(1, bs, ss, nh, 2, hd), vs.reshape(1, bs, ss, nh, 2 * hd),
            cs.reshape(1, bs, ss, -1), es.reshape(1, bs, ss, -1))
```
